```python
import jax, jax.numpy as jnp
from jax import lax
import numpy as np

D_MODEL = 1024
BATCH = 2
SEQ = 16384
DEPTH = 2

D_MIX = D_MODEL
EPS = 1e-6
GLA_HEADS = 4
GLA_DK = D_MODEL // 32
GLA_DV = D_MODEL // 16
GLA_QK = GLA_HEADS * GLA_DK
GLA_W = GLA_HEADS * GLA_DV
GLA_RANK = 16
GLA_TAU = 16.0
GLA_CHUNK = 64
FNET_GROUPS = 4
FNET_DG = D_MODEL // 16
FNET_W = FNET_GROUPS * FNET_DG
MLA_HEADS = 8
MLA_NOPE = D_MODEL // 16
MLA_ROPE = D_MODEL // 32
MLA_V = D_MODEL // 16
MLA_Q_RANK = D_MODEL // 4
MLA_KV_RANK = D_MODEL // 8
MLA_W = MLA_HEADS * MLA_V
ROPE_THETA = 10000.0
Q_BLOCK = 128
IN_SPLITS = (GLA_QK, GLA_QK, GLA_W, GLA_W, GLA_RANK, GLA_RANK, FNET_W, MLA_Q_RANK, MLA_KV_RANK, MLA_ROPE)
IN_WIDTH = GLA_QK * 2 + GLA_W * 2 + GLA_RANK * 2 + FNET_W + MLA_Q_RANK + MLA_KV_RANK + MLA_ROPE
N_GROUPS = 4
EXPERTS_PER_GROUP = 8
N_EXPERTS = N_GROUPS * EXPERTS_PER_GROUP
TOP_K = 2
D_EXPERT = D_MODEL // 4

kernel_name = "hybrid_gla_fnet_mla_hmoe_encoder"


def rms_norm(x, g):
    xf = x.astype(jnp.float32)
    y = xf * lax.rsqrt(jnp.mean(xf * xf, axis=-1, keepdims=True) + EPS)
    return (y * g.astype(jnp.float32)).astype(x.dtype)


def rope_tables(positions):
    inv = 1.0 / (ROPE_THETA ** (jnp.arange(0, MLA_ROPE, 2, dtype=jnp.float32) / MLA_ROPE))
    ang = positions.astype(jnp.float32)[..., None] * inv
    return jnp.cos(ang), jnp.sin(ang)


def apply_rope(x, cos, sin):
    x1, x2 = jnp.split(x.astype(jnp.float32), 2, axis=-1)
    return jnp.concatenate([x1 * cos - x2 * sin, x1 * sin + x2 * cos], axis=-1).astype(x.dtype)


def gla_direction(q, k, v, log_a, inclusive):
    B, S, H, dk = q.shape
    dv = v.shape[-1]
    C = GLA_CHUNK
    N = S // C
    f32 = jnp.float32
    qc = q.astype(f32).reshape(B, N, C, H, dk)
    kc = k.astype(f32).reshape(B, N, C, H, dk)
    vc = v.astype(f32).reshape(B, N, C, H, dv)
    b = jnp.cumsum(log_a.reshape(B, N, C, H, dk), axis=2)
    b_last = b[:, :, -1]
    q_dec = qc * jnp.exp(b)
    k_inv = kc * jnp.exp(-b)
    k_end = kc * jnp.exp(b_last[:, :, None] - b)
    mask = jnp.tril(jnp.ones((C, C), dtype=bool), 0 if inclusive else -1)
    att = jnp.where(mask, jnp.einsum('bnihd,bnjhd->bnhij', q_dec, k_inv), 0.0)
    o_intra = jnp.einsum('bnhij,bnjhv->bnihv', att, vc)
    kv = jnp.einsum('bnjhd,bnjhv->nbhdv', k_end, vc)
    decay = jnp.exp(b_last).transpose(1, 0, 2, 3)

    def step(state, inp):
        kv_n, dec_n = inp
        return dec_n[..., None] * state + kv_n, state

    _, s_prev = lax.scan(step, jnp.zeros((B, H, dk, dv), f32), (kv, decay))
    o_inter = jnp.einsum('bnihd,nbhdv->bnihv', q_dec, s_prev)
    return (o_intra + o_inter).reshape(B, S, H, dv)


def gla_mixer(q, k, v, g, a_f, a_b, wa_f, ba_f, wa_b, ba_b, norm_g):
    B, S, _ = q.shape
    qh = q.reshape(B, S, GLA_HEADS, GLA_DK) * (GLA_DK ** -0.5)
    kh = k.reshape(B, S, GLA_HEADS, GLA_DK)
    vh = v.reshape(B, S, GLA_HEADS, GLA_DV)
    log_af = (jax.nn.log_sigmoid((a_f @ wa_f + ba_f).astype(jnp.float32)) / GLA_TAU).reshape(B, S, GLA_HEADS, GLA_DK)
    log_ab = (jax.nn.log_sigmoid((a_b @ wa_b + ba_b).astype(jnp.float32)) / GLA_TAU).reshape(B, S, GLA_HEADS, GLA_DK)
    o_f = gla_direction(qh, kh, vh, log_af, True)
    o_b = jnp.flip(gla_direction(jnp.flip(qh, 1), jnp.flip(kh, 1), jnp.flip(vh, 1), jnp.flip(log_ab, 1), False), 1)
    o = rms_norm(o_f + o_b, norm_g).reshape(B, S, GLA_W)
    return (o * jax.nn.silu(g.astype(jnp.float32))).astype(q.dtype)


def fnet_mixer(u, w_mix):
    B, S, _ = u.shape
    uf = u.astype(jnp.float32).reshape(B, S, FNET_GROUPS, FNET_DG)
    f = jnp.fft.fft2(uf, axes=(1, 3), norm='ortho').real
    y = jnp.einsum('bsgc,gcd->bsgd', f, w_mix.astype(jnp.float32))
    return y.reshape(B, S, FNET_W).astype(u.dtype)


def mla_mixer(c_q, c_kv, k_r, cos, sin, q_norm_g, w_uq, kv_norm_g, w_ukv):
    B, S, _ = c_q.shape
    q = (rms_norm(c_q, q_norm_g) @ w_uq).reshape(B, S, MLA_HEADS, MLA_NOPE + MLA_ROPE)
    q_nope = q[..., :MLA_NOPE]
    q_rope = apply_rope(q[..., MLA_NOPE:], cos[:, :, None], sin[:, :, None])
    kv = (rms_norm(c_kv, kv_norm_g) @ w_ukv).reshape(B, S, MLA_HEADS, MLA_NOPE + MLA_V)
    k_nope = kv[..., :MLA_NOPE]
    v = kv[..., MLA_NOPE:]
    k_rope = apply_rope(k_r, cos, sin)
    scale = (MLA_NOPE + MLA_ROPE) ** -0.5
    nb = S // Q_BLOCK
    qn_blocks = q_nope.reshape(B, nb, Q_BLOCK, MLA_HEADS, MLA_NOPE).transpose(1, 0, 2, 3, 4)
    qr_blocks = q_rope.reshape(B, nb, Q_BLOCK, MLA_HEADS, MLA_ROPE).transpose(1, 0, 2, 3, 4)

    def attend(blk):
        qn, qr = blk
        s = (jnp.einsum('bqhd,bkhd->bhqk', qn, k_nope, preferred_element_type=jnp.float32)
             + jnp.einsum('bqhr,bkr->bhqk', qr, k_rope, preferred_element_type=jnp.float32)) * scale
        p = jax.nn.softmax(s, axis=-1)
        return jnp.einsum('bhqk,bkhv->bqhv', p.astype(v.dtype), v)

    o = lax.map(attend, (qn_blocks, qr_blocks))
    return o.transpose(1, 0, 2, 3, 4).reshape(B, S, MLA_W)


def hier_moe(h, w_rg, b_rg, w_re, b_re, w_gate, w_up, w_down):
    B, S, D = h.shape
    t = h.reshape(B * S, D)
    f32 = jnp.float32
    grp_p = jax.nn.softmax((t @ w_rg + b_rg).astype(f32), axis=-1)
    grp_w, grp_idx = lax.top_k(grp_p, 1)
    exp_logits = (t @ w_re + b_re).astype(f32).reshape(-1, N_GROUPS, EXPERTS_PER_GROUP)
    in_grp = jnp.einsum('tg,tge->te', jax.nn.one_hot(grp_idx[:, 0], N_GROUPS, dtype=f32), exp_logits)
    e_w, e_idx = lax.top_k(jax.nn.softmax(in_grp, axis=-1), TOP_K)
    e_w = e_w / jnp.sum(e_w, axis=-1, keepdims=True)
    expert_id = grp_idx * EXPERTS_PER_GROUP + e_idx
    combine = jnp.sum(jax.nn.one_hot(expert_id, N_EXPERTS, dtype=f32) * (grp_w * e_w)[..., None], axis=1)
    y = jnp.zeros(t.shape, f32)
    for e in range(N_EXPERTS):
        he = jax.nn.silu(t @ w_gate[e]) * (t @ w_up[e])
        y = y + combine[:, e:e + 1] * (he @ w_down[e]).astype(f32)
    return y.reshape(B, S, D).astype(h.dtype)


def setup_inputs(seed: int = 0) -> dict:
    key = jax.random.key(seed)
    ks = jax.random.split(key, 32)
    nrm = lambda k, shape, s: jax.random.normal(k, shape, jnp.float32) * s
    gain = lambda k, shape: 1.0 + 0.02 * jax.random.normal(k, shape, jnp.float32)
    L = DEPTH
    x = jax.random.normal(ks[0], (BATCH, SEQ, D_MODEL), jnp.float32)
    positions = (jnp.arange(SEQ, dtype=jnp.int32)[None, :]
                 + jax.random.randint(ks[1], (BATCH, 1), 0, 1024, dtype=jnp.int32))
    return {
        "x": x,
        "positions": positions,
        "norm1_g": gain(ks[2], (L, D_MODEL)),
        "w_in": nrm(ks[3], (L, D_MODEL, IN_WIDTH), D_MODEL ** -0.5),
        "gla_wa_f": nrm(ks[4], (L, GLA_RANK, GLA_QK), GLA_RANK ** -0.5),
        "gla_ba_f": nrm(ks[5], (L, GLA_QK), 0.1),
        "gla_wa_b": nrm(ks[6], (L, GLA_RANK, GLA_QK), GLA_RANK ** -0.5),
        "gla_ba_b": nrm(ks[7], (L, GLA_QK), 0.1),
        "gla_norm_g": gain(ks[8], (L, GLA_DV)),
        "fnet_w": nrm(ks[9], (L, FNET_GROUPS, FNET_DG, FNET_DG), FNET_DG ** -0.5),
        "mla_q_norm_g": gain(ks[10], (L, MLA_Q_RANK)),
        "mla_w_uq": nrm(ks[11], (L, MLA_Q_RANK, MLA_HEADS * (MLA_NOPE + MLA_ROPE)), MLA_Q_RANK ** -0.5),
        "mla_kv_norm_g": gain(ks[12], (L, MLA_KV_RANK)),
        "mla_w_ukv": nrm(ks[13], (L, MLA_KV_RANK, MLA_HEADS * (MLA_NOPE + MLA_V)), MLA_KV_RANK ** -0.5),
        "w_out": nrm(ks[14], (L, D_MIX, D_MODEL), D_MIX ** -0.5),
        "norm2_g": gain(ks[15], (L, D_MODEL)),
        "moe_w_rg": nrm(ks[16], (L, D_MODEL, N_GROUPS), D_MODEL ** -0.5),
        "moe_b_rg": nrm(ks[17], (L, N_GROUPS), 0.01),
        "moe_w_re": nrm(ks[18], (L, D_MODEL, N_EXPERTS), D_MODEL ** -0.5),
        "moe_b_re": nrm(ks[19], (L, N_EXPERTS), 0.01),
        "moe_w_gate": nrm(ks[20], (L, N_EXPERTS, D_MODEL, D_EXPERT), D_MODEL ** -0.5),
        "moe_w_up": nrm(ks[21], (L, N_EXPERTS, D_MODEL, D_EXPERT), D_MODEL ** -0.5),
        "moe_w_down": nrm(ks[22], (L, N_EXPERTS, D_EXPERT, D_MODEL), D_EXPERT ** -0.5),
        "final_norm_g": gain(ks[23], (D_MODEL,)),
    }


def reference(x, positions, norm1_g, w_in, gla_wa_f, gla_ba_f, gla_wa_b, gla_ba_b, gla_norm_g, fnet_w,
              mla_q_norm_g, mla_w_uq, mla_kv_norm_g, mla_w_ukv, w_out, norm2_g, moe_w_rg, moe_b_rg,
              moe_w_re, moe_b_re, moe_w_gate, moe_w_up, moe_w_down, final_norm_g):
    cos, sin = rope_tables(positions)
    split_at = [int(i) for i in np.cumsum(IN_SPLITS)[:-1]]
    for l in range(DEPTH):
        xn = rms_norm(x, norm1_g[l])
        proj = xn @ w_in[l]
        gq, gk, gv, gg, gaf, gab, fu, cq, ckv, kr = jnp.split(proj, split_at, axis=-1)
        o_gla = gla_mixer(gq, gk, gv, gg, gaf, gab, gla_wa_f[l], gla_ba_f[l], gla_wa_b[l], gla_ba_b[l], gla_norm_g[l])
        o_fnet = fnet_mixer(fu, fnet_w[l])
        o_mla = mla_mixer(cq, ckv, kr, cos, sin, mla_q_norm_g[l], mla_w_uq[l], mla_kv_norm_g[l], mla_w_ukv[l])
        x = x + jnp.concatenate([o_gla, o_fnet, o_mla], axis=-1) @ w_out[l]
        x = x + hier_moe(rms_norm(x, norm2_g[l]), moe_w_rg[l], moe_b_rg[l], moe_w_re[l], moe_b_re[l],
                         moe_w_gate[l], moe_w_up[l], moe_w_down[l])
    return rms_norm(x, final_norm_g)
```

```python
import functools
import math

import numpy as np
import jax
import jax.numpy as jnp
from jax import lax
from jax.experimental import pallas as pl
from jax.experimental.pallas import tpu as pltpu

F32 = jnp.float32
BF16 = jnp.bfloat16
EPS = 1e-6

GLA_HEADS, GLA_DK, GLA_DV, GLA_QK, GLA_W, GLA_RANK = 4, 32, 64, 128, 256, 16
GLA_TAU, GLA_CHUNK = 16.0, 64
FNET_GROUPS, FNET_DG, FNET_W = 4, 64, 256
MLA_HEADS, MLA_NOPE, MLA_ROPE, MLA_V = 8, 64, 32, 64
MLA_Q_RANK, MLA_KV_RANK, MLA_W = 256, 128, 512
MLA_QK_PAD = 128
MLA_V_ROWS = 80
ROPE_THETA = 10000.0
N_GROUPS, EXPERTS_PER_GROUP, N_EXPERTS, D_EXPERT = 4, 8, 32, 256

SEG_ALIGN = 16
ROW_TILE = 128
VMEM_LIMIT = 56 * 1024 * 1024

NT_DIMS = (((1,), (1,)), ((), ()))
TN_DIMS = (((0,), (0,)), ((), ()))


def _dot(a, b):
    return jnp.dot(a, b, preferred_element_type=F32)


def _dot_hi(a, b):
    return jnp.dot(a, b, preferred_element_type=F32, precision=lax.Precision.HIGHEST)


def _rms(x, g):
    return x * lax.rsqrt(jnp.mean(x * x, axis=-1, keepdims=True) + EPS) * g


def _params(sem):
    return pltpu.CompilerParams(dimension_semantics=sem, vmem_limit_bytes=VMEM_LIMIT)


def _full(shape):
    return pl.BlockSpec(shape, lambda *_: (0,) * len(shape))


def _proj_kernel(has_res, *refs):
    if has_res:
        x_ref, y_ref = refs[:2]
        refs = refs[2:]
    else:
        x_ref = refs[0]
        refs = refs[1:]
    (g1_ref, w1_ref, wab_ref, bab_ref, cbd_ref, sbd_ref, wmix_ref, qg_ref, wq1_ref, wq2_ref,
     cost_ref, sint_ref, kvg_ref, wk_ref, cosk_ref, sink_ref, wvt_ref) = refs[:17]
    outs = refs[17:]
    if has_res:
        xs_ref = outs[0]
        outs = outs[1:]
    (gq_ref, gk_ref, gv_ref, gg_ref, laf_ref, lab_ref, zr_ref, zi_ref, qt_ref, kk_ref, vt_ref) = outs

    x = x_ref[0]
    if has_res:
        x = x + y_ref[0]
        xs_ref[0] = x
    xn = _rms(x, g1_ref[...]).astype(BF16)
    p = _dot(xn, w1_ref[...])
    gq_ref[0] = (p[:, 0:128] * (GLA_DK ** -0.5)).astype(BF16)
    gk_ref[0] = p[:, 128:256].astype(BF16)
    gv_ref[0] = p[:, 256:512].astype(BF16)
    gg_ref[0] = p[:, 512:768].astype(BF16)
    pre = _dot(p[:, 768:896].astype(BF16), wab_ref[...]) + bab_ref[...]
    la = (jnp.minimum(pre, 0.0) - jnp.log(1.0 + jnp.exp(-jnp.abs(pre)))) * (1.0 / GLA_TAU)
    laf_ref[0] = la[:, 0:128]
    lab_ref[0] = la[:, 128:256]
    u = p[:, 896:1152].astype(BF16)
    zr_ref[0] = _dot(_dot(u, cbd_ref[...]).astype(BF16), wmix_ref[...]).astype(BF16)
    zi_ref[0] = (-_dot(_dot(u, sbd_ref[...]).astype(BF16), wmix_ref[...])).astype(BF16)
    cqn = _rms(p[:, 1152:1408], qg_ref[...]).astype(BF16)
    qa = lax.dot_general(wq1_ref[...], cqn, NT_DIMS, preferred_element_type=F32)
    qb = lax.dot_general(wq2_ref[...], cqn, NT_DIMS, preferred_element_type=F32)
    cos_t = cost_ref[0]
    sin_t = sint_ref[0]
    scale = (MLA_NOPE + MLA_ROPE) ** -0.5
    for h in range(MLA_HEADS):
        sl = slice(h * MLA_QK_PAD, (h + 1) * MLA_QK_PAD)
        qt_ref[0, h] = ((qa[sl] * cos_t + qb[sl] * sin_t) * scale).astype(BF16)
    kvn = _rms(p[:, 1408:1536], kvg_ref[...]).astype(BF16)
    kf = _dot(kvn, wk_ref[...])
    krope = p[:, 1536:1664] * cosk_ref[0] + p[:, 1664:1792] * sink_ref[0]
    for h in range(MLA_HEADS):
        sl = slice(h * MLA_QK_PAD, (h + 1) * MLA_QK_PAD)
        kk_ref[0, h] = (kf[:, sl] + krope).astype(BF16)
    vt = lax.dot_general(wvt_ref[...], kvn, NT_DIMS, preferred_element_type=F32)
    tm = vt.shape[1]
    ones = jnp.ones((MLA_V_ROWS - MLA_V, tm), BF16)
    for h in range(MLA_HEADS):
        vt_ref[0, h, 0:MLA_V, :] = vt[h * MLA_V:(h + 1) * MLA_V].astype(BF16)
        vt_ref[0, h, MLA_V:MLA_V_ROWS, :] = ones


def _proj(x, y, lw, tabs, tm):
    B, S, D = x.shape
    has_res = y is not None
    H = MLA_HEADS
    tok = lambda w: pl.BlockSpec((1, tm, w), lambda b, i: (b, i, 0))
    in_specs = [tok(D)] + ([tok(D)] if has_res else [])
    weights = [lw["g1"], lw["w1"], lw["wab"], lw["bab"], tabs["cbd"], tabs["sbd"], lw["wmix"], lw["qg"],
               lw["wq1"], lw["wq2"]]
    in_specs += [_full(w.shape) for w in weights]
    in_specs += [pl.BlockSpec((1, MLA_QK_PAD, tm), lambda b, i: (b, 0, i))] * 2
    in_specs += [_full(lw["kvg"].shape), _full(lw["wk"].shape)]
    in_specs += [tok(MLA_QK_PAD)] * 2
    in_specs += [_full(lw["wvt"].shape)]
    args = [x] + ([y] if has_res else []) + weights + [tabs["cos_t"], tabs["sin_t"], lw["kvg"], lw["wk"],
                                                       tabs["cos_k"], tabs["sin_k"], lw["wvt"]]
    out_shape, out_specs = [], []
    if has_res:
        out_shape.append(jax.ShapeDtypeStruct((B, S, D), F32))
        out_specs.append(tok(D))
    for w, dt in ((128, BF16), (128, BF16), (256, BF16), (256, BF16), (128, F32), (128, F32), (256, BF16),
                  (256, BF16)):
        out_shape.append(jax.ShapeDtypeStruct((B, S, w), dt))
        out_specs.append(tok(w))
    out_shape.append(jax.ShapeDtypeStruct((B, H, MLA_QK_PAD, S), BF16))
    out_specs.append(pl.BlockSpec((1, H, MLA_QK_PAD, tm), lambda b, i: (b, 0, 0, i)))
    out_shape.append(jax.ShapeDtypeStruct((B, H, S, MLA_QK_PAD), BF16))
    out_specs.append(pl.BlockSpec((1, H, tm, MLA_QK_PAD), lambda b, i: (b, 0, i, 0)))
    out_shape.append(jax.ShapeDtypeStruct((B, H, MLA_V_ROWS, S), BF16))
    out_specs.append(pl.BlockSpec((1, H, MLA_V_ROWS, tm), lambda b, i: (b, 0, 0, i)))
    res = pl.pallas_call(
        functools.partial(_proj_kernel, has_res),
        grid=(B, S // tm), in_specs=in_specs, out_specs=out_specs, out_shape=out_shape,
        compiler_params=_params(("parallel", "parallel")), name="proj")(*args)
    res = list(res)
    xs = res.pop(0) if has_res else x
    return xs, res


def _gla_consts():
    C, W, QK = GLA_CHUNK, GLA_W, GLA_QK
    i64 = lax.broadcasted_iota(jnp.int32, (C, C), 0)
    j64 = lax.broadcasted_iota(jnp.int32, (C, C), 1)
    r_att = lax.broadcasted_iota(jnp.int32, (C, W), 0)
    c_att = lax.broadcasted_iota(jnp.int32, (C, W), 1) % C
    r_ks = lax.broadcasted_iota(jnp.int32, (W, QK), 0) // C
    c_ks = lax.broadcasted_iota(jnp.int32, (W, QK), 1) // GLA_DK
    r_bd = lax.broadcasted_iota(jnp.int32, (W, W), 0) // C
    c_bd = lax.broadcasted_iota(jnp.int32, (W, W), 1) // GLA_DV
    r_s = lax.broadcasted_iota(jnp.int32, (QK, W), 0) // GLA_DK
    c_s = lax.broadcasted_iota(jnp.int32, (QK, W), 1) // GLA_DV
    return dict(
        tri_f=(j64 <= i64).astype(F32), tri_b=(j64 >= i64).astype(F32),
        att_f=c_att <= r_att, att_b=c_att > r_att,
        ks=r_ks == c_ks, bd=r_bd == c_bd, st=r_s == c_s)


def _gla_chunk(q, k, v, la, state, fwd, cs):
    C = GLA_CHUNK
    b = _dot_hi(cs["tri_f"] if fwd else cs["tri_b"], la)
    b_edge = b[C - 1:C] if fwd else b[0:1]
    qd = (q * jnp.exp(b)).astype(BF16)
    ki = (k * jnp.exp(-b)).astype(BF16)
    ke = (k * jnp.exp(b_edge - b)).astype(BF16)
    kstack = jnp.where(cs["ks"], jnp.concatenate([ki] * GLA_HEADS, axis=0), jnp.zeros((), BF16))
    att = lax.dot_general(qd, kstack, NT_DIMS, preferred_element_type=F32)
    att = jnp.where(cs["att_f"] if fwd else cs["att_b"], att, 0.0).astype(BF16)
    vbd = jnp.where(cs["bd"], jnp.concatenate([v] * GLA_HEADS, axis=0), jnp.zeros((), BF16))
    o = _dot(att, vbd) + _dot(qd, state.astype(BF16))
    kv = lax.dot_general(ke, v, TN_DIMS, preferred_element_type=F32)
    dec = jnp.exp(jnp.broadcast_to(b_edge, (GLA_QK, GLA_QK)).T)
    state = jnp.concatenate([dec, dec], axis=1) * state + jnp.where(cs["st"], kv, 0.0)
    return o, state


def _gla_kernel(qf_ref, kf_ref, vf_ref, laf_ref, qb_ref, kb_ref, vb_ref, lab_ref, of_ref, ob_ref,
                sf_ref, sb_ref, *, nchunk):
    C = GLA_CHUNK

    @pl.when(pl.program_id(1) == 0)
    def _():
        sf_ref[...] = jnp.zeros_like(sf_ref)
        sb_ref[...] = jnp.zeros_like(sb_ref)

    cs = _gla_consts()

    def body(c, carry):
        s_f, s_b = carry
        rf = pl.ds(pl.multiple_of(c * C, C), C)
        o, s_f = _gla_chunk(qf_ref[0, rf, :].astype(F32), kf_ref[0, rf, :].astype(F32), vf_ref[0, rf, :],
                            laf_ref[0, rf, :], s_f, True, cs)
        of_ref[0, rf, :] = o
        rb = pl.ds(pl.multiple_of((nchunk - 1 - c) * C, C), C)
        o, s_b = _gla_chunk(qb_ref[0, rb, :].astype(F32), kb_ref[0, rb, :].astype(F32), vb_ref[0, rb, :],
                            lab_ref[0, rb, :], s_b, False, cs)
        ob_ref[0, rb, :] = o
        return s_f, s_b

    s_f, s_b = lax.fori_loop(0, nchunk, body, (sf_ref[...], sb_ref[...]))
    sf_ref[...] = s_f
    sb_ref[...] = s_b


def _gla(gq, gk, gv, la_f, la_b, tg):
    B, S, _ = gq.shape
    nb = S // tg
    fw = lambda w: pl.BlockSpec((1, tg, w), lambda b, i: (b, i, 0))
    bw = lambda w: pl.BlockSpec((1, tg, w), lambda b, i: (b, nb - 1 - i, 0))
    return pl.pallas_call(
        functools.partial(_gla_kernel, nchunk=tg // GLA_CHUNK),
        grid=(B, nb),
        in_specs=[fw(128), fw(128), fw(256), fw(128), bw(128), bw(128), bw(256), bw(128)],
        out_specs=[fw(256), bw(256)],
        out_shape=[jax.ShapeDtypeStruct((B, S, GLA_W), F32)] * 2,
        scratch_shapes=[pltpu.VMEM((GLA_QK, GLA_W), F32)] * 2,
        compiler_params=_params(("parallel", "arbitrary")), name="gla")(gq, gk, gv, la_f, gq, gk, gv, la_b)


def _fnet1_kernel(zr_ref, zi_ref, m1_ref, tc_ref, ts_ref, or_ref, oi_ref, *, ts1):
    n2 = zr_ref.shape[1]
    x = jnp.concatenate([zr_ref[0], zi_ref[0]], axis=0)
    z = _dot(m1_ref[...], x)
    zr, zi = z[:n2], z[n2:]
    tc, tsn = tc_ref[...], ts_ref[...]
    pr = zr * tc + zi * tsn
    pi = zi * tc - zr * tsn
    for j in range(ts1):
        sl = slice(j * FNET_W, (j + 1) * FNET_W)
        or_ref[0, j] = pr[:, sl].astype(BF16)
        oi_ref[0, j] = pi[:, sl].astype(BF16)


def _fnet2_kernel(zr_ref, zi_ref, m2_ref, o_ref):
    x = jnp.concatenate([zr_ref[0], zi_ref[0]], axis=0)
    o_ref[0] = _dot(m2_ref[...], x).astype(BF16)


def _fnet(zr, zi, tabs, ts1, tk2):
    B, S, CH = zr.shape
    n1, n2 = tabs["n1"], tabs["n2"]
    zr2 = zr.reshape(B, n2, n1 * CH)
    zi2 = zi.reshape(B, n2, n1 * CH)
    cols = ts1 * CH
    in_b = pl.BlockSpec((1, n2, cols), lambda b, i: (b, 0, i))
    tw_b = pl.BlockSpec((n2, cols), lambda b, i: (0, i))
    out_b = pl.BlockSpec((1, ts1, n2, CH), lambda b, i: (b, i, 0, 0))
    pr, pi = pl.pallas_call(
        functools.partial(_fnet1_kernel, ts1=ts1),
        grid=(B, n1 // ts1),
        in_specs=[in_b, in_b, _full(tabs["m1"].shape), tw_b, tw_b],
        out_specs=[out_b, out_b],
        out_shape=[jax.ShapeDtypeStruct((B, n1, n2, CH), BF16)] * 2,
        compiler_params=_params(("parallel", "parallel")), name="fnet1")(zr2, zi2, tabs["m1"], tabs["tw_c"],
                                                                         tabs["tw_s"])
    pr = pr.reshape(B, n1, n2 * CH)
    pi = pi.reshape(B, n1, n2 * CH)
    cols2 = tk2 * CH
    blk = pl.BlockSpec((1, n1, cols2), lambda b, i: (b, 0, i))
    out = pl.pallas_call(
        _fnet2_kernel, grid=(B, n2 // tk2),
        in_specs=[blk, blk, _full(tabs["m2"].shape)], out_specs=blk,
        out_shape=jax.ShapeDtypeStruct((B, n1, n2 * CH), BF16),
        compiler_params=_params(("parallel", "parallel")), name="fnet2")(pr, pi, tabs["m2"])
    return out.reshape(B, S, CH)


def _attn_kernel(qt_ref, k_ref, vt_ref, o_ref, *, tk, nk):
    qt = qt_ref[0, 0]
    tq = qt.shape[1]

    def body(j, carry):
        m, acc = carry
        ks = pl.ds(pl.multiple_of(j * tk, tk), tk)
        s = _dot(k_ref[0, 0, ks, :], qt)
        m_new = jnp.maximum(m, jnp.max(s, axis=0, keepdims=True))
        p = jnp.exp(s - m_new).astype(BF16)
        acc = jnp.exp(m - m_new) * acc + _dot(vt_ref[0, 0, :, ks], p)
        return m_new, acc

    m0 = jnp.full((1, tq), -1e30, F32)
    acc0 = jnp.zeros((MLA_V_ROWS, tq), F32)
    _, acc = lax.fori_loop(0, nk, body, (m0, acc0))
    o_ref[0] = (acc[0:MLA_V] / acc[MLA_V:MLA_V + 1]).astype(BF16)


def _attn(qt, kk, vt, tq, tk):
    B, H, _, S = qt.shape
    return pl.pallas_call(
        functools.partial(_attn_kernel, tk=tk, nk=S // tk),
        grid=(B, H, S // tq),
        in_specs=[pl.BlockSpec((1, 1, MLA_QK_PAD, tq), lambda b, h, i: (b, h, 0, i)),
                  pl.BlockSpec((1, 1, S, MLA_QK_PAD), lambda b, h, i: (b, h, 0, 0)),
                  pl.BlockSpec((1, 1, MLA_V_ROWS, S), lambda b, h, i: (b, h, 0, 0))],
        out_specs=pl.BlockSpec((1, MLA_V, tq), lambda b, h, i: (b, h, i)),
        out_shape=jax.ShapeDtypeStruct((B, H * MLA_V, S), BF16),
        compiler_params=_params(("parallel", "parallel", "arbitrary")), name="attn")(qt, kk, vt)


def _outproj_kernel(x_ref, of_ref, ob_ref, gg_ref, gng_ref, hm_ref, yf_ref, ot_ref, wo_ref, n2g_ref,
                    wrh_ref, wrl_ref, br_ref, ut_ref, lt_ref,
                    x1_ref, h_ref, info_ref, off_ref, cnt_ref):
    o = of_ref[0] + ob_ref[0]
    msq = _dot((o * o).astype(BF16), hm_ref[...])
    gate = gg_ref[0].astype(F32)
    og = o * lax.rsqrt(msq + EPS) * gng_ref[...] * (gate * jax.nn.sigmoid(gate))
    acc = _dot(og.astype(BF16), wo_ref[0:GLA_W, :])
    acc += _dot(yf_ref[0], wo_ref[GLA_W:GLA_W + FNET_W, :])
    acc += lax.dot_general(ot_ref[0], wo_ref[GLA_W + FNET_W:, :], TN_DIMS, preferred_element_type=F32)
    x1 = x_ref[0] + acc
    x1_ref[0] = x1
    h = _rms(x1, n2g_ref[...])
    h_hi = h.astype(BF16)
    h_ref[0] = h_hi
    h_lo = (h - h_hi.astype(F32)).astype(BF16)
    nt = lambda w, a: lax.dot_general(w, a, NT_DIMS, preferred_element_type=F32)
    lg = nt(wrh_ref[...], h_hi) + nt(wrh_ref[...], h_lo) + nt(wrl_ref[...], h_hi) + br_ref[...]
    tb = lg.shape[1]
    neg = jnp.float32(-jnp.inf)
    row8 = lax.broadcasted_iota(jnp.int32, (8, tb), 0)
    g8 = lg[128:136]
    gm = jnp.where(row8 < N_GROUPS, g8, neg)
    gmax = jnp.max(gm, axis=0, keepdims=True)
    gidx = jnp.min(jnp.where(gm == gmax, row8, 99), axis=0, keepdims=True)
    gsum = jnp.sum(jnp.where(row8 < N_GROUPS, jnp.exp(g8 - gmax), 0.0), axis=0, keepdims=True)
    grp_w = 1.0 / gsum
    row = lax.broadcasted_iota(jnp.int32, (N_EXPERTS, tb), 0)
    em = jnp.where((row // EXPERTS_PER_GROUP) == gidx, lg[0:N_EXPERTS], neg)
    e1 = jnp.max(em, axis=0, keepdims=True)
    i1 = jnp.min(jnp.where(em == e1, row, 99), axis=0, keepdims=True)
    em2 = jnp.where(row == i1, neg, em)
    e2 = jnp.max(em2, axis=0, keepdims=True)
    i2 = jnp.min(jnp.where(em2 == e2, row, 99), axis=0, keepdims=True)
    r = jnp.exp(e2 - e1)
    c1 = grp_w / (1.0 + r)
    c2 = grp_w * r / (1.0 + r)
    sel1 = row == i1
    sel2 = row == i2
    oh = jnp.where(sel1 | sel2, 1.0, 0.0)
    rank = _dot(oh.astype(BF16), ut_ref[...])
    cnt = jnp.sum(oh, axis=1, keepdims=True)
    padded = jnp.ceil(cnt * (1.0 / SEG_ALIGN)) * SEG_ALIGN
    off = _dot_hi(lt_ref[...], jnp.broadcast_to(padded, (N_EXPERTS, 128)))
    pos = off[:, 0:1] + rank
    pos1 = jnp.sum(jnp.where(sel1, pos, 0.0), axis=0, keepdims=True)
    pos2 = jnp.sum(jnp.where(sel2, pos, 0.0), axis=0, keepdims=True)
    info_ref[0] = jnp.concatenate([pos1, pos2, c1, c2, jnp.zeros((4, tb), F32)], axis=0)
    off_ref[0] = off.astype(jnp.int32)
    cnt_ref[0] = jnp.broadcast_to(cnt, (N_EXPERTS, 128)).astype(jnp.int32)


def _outproj(x, o_f, o_b, gg, yf, ot, lw, tabs, tb):
    B, S, D = x.shape
    nsb = S // tb
    tok = lambda w: pl.BlockSpec((1, tb, w), lambda b, i: (b, i, 0))
    weights_a = [lw["gng"], tabs["hm"]]
    weights_b = [lw["wo"], lw["n2g"], lw["wrh"], lw["wrl"], lw["br"], tabs["ut"], tabs["lt"]]
    in_specs = ([tok(D), tok(GLA_W), tok(GLA_W), tok(GLA_W)] + [_full(w.shape) for w in weights_a]
                + [tok(FNET_W), pl.BlockSpec((1, MLA_W, tb), lambda b, i: (b, 0, i))]
                + [_full(w.shape) for w in weights_b])
    blk = lambda r, c, dt: (jax.ShapeDtypeStruct((B * nsb, r, c), dt),
                            pl.BlockSpec((1, r, c), lambda b, i: (b * nsb + i, 0, 0)))
    outs = [(jax.ShapeDtypeStruct((B, S, D), F32), tok(D)), (jax.ShapeDtypeStruct((B, S, D), BF16), tok(D)),
            blk(8, tb, F32), blk(N_EXPERTS, 128, jnp.int32), blk(N_EXPERTS, 128, jnp.int32)]
    return pl.pallas_call(
        _outproj_kernel, grid=(B, nsb), in_specs=in_specs,
        out_specs=[o[1] for o in outs], out_shape=[o[0] for o in outs],
        compiler_params=_params(("parallel", "parallel")), name="outproj")(
            x, o_f, o_b, gg, *weights_a, yf, ot, *weights_b)


def _moe_kernel(off_ref, cnt_ref, h_ref, info_ref, wg_ref, wu_ref, wd_ref, y_ref,
                pm_ref, xs_ref, ow_ref, cs_ref):
    blk = pl.program_id(0)
    e = pl.program_id(1)
    ns, tb = pm_ref.shape

    @pl.when(e == 0)
    def _():
        info = info_ref[0]
        pos1 = info[0:1].astype(jnp.int32)
        pos2 = info[1:2].astype(jnp.int32)
        slot = lax.broadcasted_iota(jnp.int32, (ns, tb), 0)
        p1 = slot == pos1
        p2 = slot == pos2
        pm = jnp.where(p1 | p2, 1.0, 0.0).astype(BF16)
        pm_ref[...] = pm
        cs_ref[...] = jnp.sum(jnp.where(p1, info[2:3], 0.0) + jnp.where(p2, info[3:4], 0.0), axis=1,
                              keepdims=True)
        xs_ref[...] = _dot(pm, h_ref[...]).astype(BF16)
        ow_ref[...] = jnp.zeros_like(ow_ref)

    off_e = off_ref[blk * N_EXPERTS + e]
    ntile = (cnt_ref[blk * N_EXPERTS + e] + ROW_TILE - 1) // ROW_TILE

    def tile(i, carry):
        rows = pl.ds(pl.multiple_of(off_e + i * ROW_TILE, SEG_ALIGN), ROW_TILE)
        xt = xs_ref[rows, :]
        g = _dot(xt, wg_ref[0])
        u = _dot(xt, wu_ref[0])
        hh = (g * jax.nn.sigmoid(g) * u).astype(BF16)
        o = _dot(hh, wd_ref[0])
        ow_ref[rows, :] = (o * cs_ref[rows, :]).astype(BF16)
        return carry

    lax.fori_loop(0, ntile, tile, 0)

    @pl.when(e == N_EXPERTS - 1)
    def _():
        y_ref[...] = lax.dot_general(pm_ref[...], ow_ref[...], TN_DIMS, preferred_element_type=F32)


def _moe(h, info, off, cnt, lw, tb):
    T, D = h.shape
    nblk = T // tb
    ns = 2 * tb + N_EXPERTS * SEG_ALIGN + ROW_TILE
    grid_spec = pltpu.PrefetchScalarGridSpec(
        num_scalar_prefetch=2, grid=(nblk, N_EXPERTS),
        in_specs=[pl.BlockSpec((tb, D), lambda b, e, *_: (b, 0)),
                  pl.BlockSpec((1, 8, tb), lambda b, e, *_: (b, 0, 0)),
                  pl.BlockSpec((1, D, D_EXPERT), lambda b, e, *_: (e, 0, 0)),
                  pl.BlockSpec((1, D, D_EXPERT), lambda b, e, *_: (e, 0, 0)),
                  pl.BlockSpec((1, D_EXPERT, D), lambda b, e, *_: (e, 0, 0))],
        out_specs=pl.BlockSpec((tb, D), lambda b, e, *_: (b, 0)),
        scratch_shapes=[pltpu.VMEM((ns, tb), BF16), pltpu.VMEM((ns, D), BF16), pltpu.VMEM((ns, D), BF16),
                        pltpu.VMEM((ns, 1), F32)])
    return pl.pallas_call(
        _moe_kernel, grid_spec=grid_spec, out_shape=jax.ShapeDtypeStruct((T, D), F32),
        compiler_params=_params(("parallel", "arbitrary")), name="moe")(
            off, cnt, h, info, lw["wg"], lw["wu"], lw["wd"])


def _final_kernel(x_ref, y_ref, g_ref, o_ref):
    o_ref[...] = _rms(x_ref[...] + y_ref[...], g_ref[...])


def _final(x, y, g, tm):
    T, D = x.shape
    blk = pl.BlockSpec((tm, D), lambda i: (i, 0))
    return pl.pallas_call(
        _final_kernel, grid=(T // tm,), in_specs=[blk, blk, _full((1, D))], out_specs=blk,
        out_shape=jax.ShapeDtypeStruct((T, D), F32),
        compiler_params=_params(("parallel",)), name="final")(x, y, g)


def _block_diag(blocks):
    n = len(blocks)
    r, c = blocks[0].shape
    out = jnp.zeros((n * r, n * c), blocks[0].dtype)
    for i, b in enumerate(blocks):
        out = out.at[i * r:(i + 1) * r, i * c:(i + 1) * c].set(b)
    return out


def _tables(positions, S, tb):
    B = positions.shape[0]
    n1 = 1 << ((S.bit_length() - 1 + 1) // 2)
    n2 = S // n1
    assert n1 * n2 == S
    half = MLA_ROPE // 2
    inv = 1.0 / (ROPE_THETA ** (jnp.arange(0, MLA_ROPE, 2, dtype=F32) / MLA_ROPE))
    ang = positions.astype(F32)[..., None] * inv
    cos, sin = jnp.cos(ang), jnp.sin(ang)
    z = lambda n: jnp.zeros((B, S, n), F32)
    pad = MLA_QK_PAD - MLA_NOPE - MLA_ROPE
    cos_q = jnp.concatenate([jnp.ones((B, S, MLA_NOPE), F32), cos, cos, z(pad)], -1)
    sin_q = jnp.concatenate([z(MLA_NOPE), -sin, sin, z(pad)], -1)
    cos_k = jnp.concatenate([z(MLA_NOPE), cos, cos, z(pad)], -1)
    dft = lambda n: 2.0 * np.pi * np.outer(np.arange(n), np.arange(n)) / n
    a64 = dft(FNET_DG)
    c64 = jnp.asarray(np.cos(a64) / math.sqrt(FNET_DG), F32)
    s64 = jnp.asarray(np.sin(a64) / math.sqrt(FNET_DG), F32)
    a2, a1 = dft(n2), dft(n1)
    c2, s2 = np.cos(a2) / math.sqrt(n2), np.sin(a2) / math.sqrt(n2)
    c1, s1 = np.cos(a1) / math.sqrt(n1), np.sin(a1) / math.sqrt(n1)
    m1 = np.block([[c2, s2], [-s2, c2]])
    m2 = np.concatenate([c1, s1], axis=1)
    kk = jnp.arange(n2, dtype=jnp.int32)[:, None] * jnp.arange(n1, dtype=jnp.int32)[None, :]
    th = (kk % S).astype(F32) * (2.0 * math.pi / S)
    hm = np.kron(np.eye(GLA_HEADS), np.full((GLA_DV, GLA_DV), 1.0 / GLA_DV))
    tt = np.arange(tb)
    return dict(
        n1=n1, n2=n2,
        cos_t=cos_q.transpose(0, 2, 1), sin_t=sin_q.transpose(0, 2, 1), cos_k=cos_k, sin_k=sin_q,
        cbd=_block_diag([c64] * FNET_GROUPS).astype(BF16), sbd=_block_diag([s64] * FNET_GROUPS).astype(BF16),
        m1=jnp.asarray(m1, BF16), m2=jnp.asarray(m2, BF16),
        tw_c=jnp.repeat(jnp.cos(th), FNET_W, axis=1), tw_s=jnp.repeat(jnp.sin(th), FNET_W, axis=1),
        hm=jnp.asarray(hm, BF16),
        ut=jnp.asarray(tt[:, None] < tt[None, :], BF16),
        lt=jnp.asarray(np.arange(N_EXPERTS)[None, :] < np.arange(N_EXPERTS)[:, None], F32))


def _layer_weights(l, p):
    w_in = p["w_in"][l]
    D = w_in.shape[0]
    o = np.cumsum([0, GLA_QK, GLA_QK, GLA_W, GLA_W, GLA_RANK, GLA_RANK, FNET_W, MLA_Q_RANK, MLA_KV_RANK, MLA_ROPE])
    col = lambda i: w_in[:, int(o[i]):int(o[i + 1])]
    zc = lambda n: jnp.zeros((D, n), F32)
    half = MLA_ROPE // 2
    kr = col(9)
    pad = MLA_QK_PAD - MLA_NOPE - MLA_ROPE
    kr1 = jnp.concatenate([zc(MLA_NOPE), kr[:, :half], kr[:, half:], zc(pad)], 1)
    kr2 = jnp.concatenate([zc(MLA_NOPE), kr[:, half:], kr[:, :half], zc(pad)], 1)
    w1 = jnp.concatenate([col(0), col(1), col(2), col(3), col(4), col(5), zc(128 - 2 * GLA_RANK), col(6), col(7),
                          col(8), kr1, kr2], 1).astype(BF16)
    wab = jnp.zeros((128, 2 * GLA_QK), F32)
    wab = wab.at[0:GLA_RANK, 0:GLA_QK].set(p["gla_wa_f"][l])
    wab = wab.at[GLA_RANK:2 * GLA_RANK, GLA_QK:].set(p["gla_wa_b"][l])
    bab = jnp.concatenate([p["gla_ba_f"][l], p["gla_ba_b"][l]])[None, :]
    wuq = p["mla_w_uq"][l].reshape(MLA_Q_RANK, MLA_HEADS, MLA_NOPE + MLA_ROPE)
    zq = lambda n: jnp.zeros((MLA_Q_RANK, MLA_HEADS, n), F32)
    nope, x1, x2 = wuq[..., :MLA_NOPE], wuq[..., MLA_NOPE:MLA_NOPE + half], wuq[..., MLA_NOPE + half:]
    wq1 = jnp.concatenate([nope, x1, x2, zq(pad)], -1).reshape(MLA_Q_RANK, -1)
    wq2 = jnp.concatenate([zq(MLA_NOPE), x2, x1, zq(pad)], -1).reshape(MLA_Q_RANK, -1)
    wukv = p["mla_w_ukv"][l].reshape(MLA_KV_RANK, MLA_HEADS, MLA_NOPE + MLA_V)
    wk = jnp.concatenate([wukv[..., :MLA_NOPE], jnp.zeros((MLA_KV_RANK, MLA_HEADS, MLA_QK_PAD - MLA_NOPE), F32)],
                         -1).reshape(MLA_KV_RANK, -1)
    wvt = wukv[..., MLA_NOPE:].reshape(MLA_KV_RANK, -1).T
    wr = jnp.zeros((256, D), F32)
    wr = wr.at[0:N_EXPERTS].set(p["moe_w_re"][l].T).at[128:128 + N_GROUPS].set(p["moe_w_rg"][l].T)
    br = jnp.zeros((256, 1), F32)
    br = br.at[0:N_EXPERTS, 0].set(p["moe_b_re"][l]).at[128:128 + N_GROUPS, 0].set(p["moe_b_rg"][l])
    wrh = wr.astype(BF16)
    return dict(
        g1=p["norm1_g"][l][None, :], w1=w1, wab=wab.astype(BF16), bab=bab,
        wmix=_block_diag([p["fnet_w"][l, g] for g in range(FNET_GROUPS)]).astype(BF16),
        qg=p["mla_q_norm_g"][l][None, :], wq1=wq1.T.astype(BF16), wq2=wq2.T.astype(BF16),
        kvg=p["mla_kv_norm_g"][l][None, :], wk=wk.astype(BF16), wvt=wvt.astype(BF16),
        gng=jnp.tile(p["gla_norm_g"][l], GLA_HEADS)[None, :], wo=p["w_out"][l].astype(BF16),
        n2g=p["norm2_g"][l][None, :], wrh=wrh, wrl=(wr - wrh.astype(F32)).astype(BF16), br=br,
        wg=p["moe_w_gate"][l].astype(BF16), wu=p["moe_w_up"][l].astype(BF16), wd=p["moe_w_down"][l].astype(BF16))


def _tiles(S):
    return dict(tm=min(512, S), tg=min(512, S), ts1=8, tk2=8, tq=min(512, S), tk=min(512, S), tb=min(1024, S))


def kernel(x, positions, norm1_g, w_in, gla_wa_f, gla_ba_f, gla_wa_b, gla_ba_b, gla_norm_g, fnet_w, mla_q_norm_g,
           mla_w_uq, mla_kv_norm_g, mla_w_ukv, w_out, norm2_g, moe_w_rg, moe_b_rg, moe_w_re, moe_b_re, moe_w_gate,
           moe_w_up, moe_w_down, final_norm_g):
    p = dict(norm1_g=norm1_g, w_in=w_in, gla_wa_f=gla_wa_f, gla_ba_f=gla_ba_f, gla_wa_b=gla_wa_b,
             gla_ba_b=gla_ba_b, gla_norm_g=gla_norm_g, fnet_w=fnet_w, mla_q_norm_g=mla_q_norm_g,
             mla_w_uq=mla_w_uq, mla_kv_norm_g=mla_kv_norm_g, mla_w_ukv=mla_w_ukv, w_out=w_out, norm2_g=norm2_g,
             moe_w_rg=moe_w_rg, moe_b_rg=moe_b_rg, moe_w_re=moe_w_re, moe_b_re=moe_b_re, moe_w_gate=moe_w_gate,
             moe_w_up=moe_w_up, moe_w_down=moe_w_down)
    B, S, D = x.shape
    t = _tiles(S)
    tabs = _tables(positions, S, t["tb"])
    y = None
    for l in range(w_in.shape[0]):
        lw = _layer_weights(l, p)
        x, (gq, gk, gv, gg, la_f, la_b, zr, zi, qt, kk, vt) = _proj(x, y, lw, tabs, t["tm"])
        o_f, o_b = _gla(gq, gk, gv, la_f, la_b, t["tg"])
        yf = _fnet(zr, zi, tabs, min(t["ts1"], tabs["n1"]), min(t["tk2"], tabs["n2"]))
        ot = _attn(qt, kk, vt, t["tq"], t["tk"])
        x, h, info, off, cnt = _outproj(x, o_f, o_b, gg, yf, ot, lw, tabs, t["tb"])
        y = _moe(h.reshape(B * S, D), info, off[:, :, 0].reshape(-1), cnt[:, :, 0].reshape(-1), lw, t["tb"])
        y = y.reshape(B, S, D)
    return _final(x.reshape(B * S, D), y.reshape(B * S, D), final_norm_g[None, :], t["tm"]).reshape(B, S, D)
```

```python
import functools
import math

import numpy as np
import jax
import jax.numpy as jnp
from jax import lax
from jax.experimental import pallas as pl
from jax.experimental.pallas import tpu as pltpu

F32 = jnp.float32
BF16 = jnp.bfloat16
EPS = 1e-6

GLA_HEADS, GLA_DK, GLA_DV, GLA_QK, GLA_W, GLA_RANK = 4, 32, 64, 128, 256, 16
GLA_TAU, GLA_CHUNK = 16.0, 64
FNET_GROUPS, FNET_DG, FNET_W = 4, 64, 256
MLA_HEADS, MLA_NOPE, MLA_ROPE, MLA_V = 8, 64, 32, 64
MLA_Q_RANK, MLA_KV_RANK, MLA_W = 256, 128, 512
MLA_QK_PAD = 128
MLA_V_ROWS = 80
ROPE_THETA = 10000.0
N_GROUPS, EXPERTS_PER_GROUP, N_EXPERTS, D_EXPERT = 4, 8, 32, 256

SEG_ALIGN = 16
ROW_TILE = 128
VMEM_LIMIT = 56 * 1024 * 1024

NT_DIMS = (((1,), (1,)), ((), ()))
TN_DIMS = (((0,), (0,)), ((), ()))


def _dot(a, b):
    return jnp.dot(a, b, preferred_element_type=F32)


def _dot_hi(a, b):
    return jnp.dot(a, b, preferred_element_type=F32, precision=lax.Precision.HIGHEST)


def _rms(x, g):
    return x * lax.rsqrt(jnp.mean(x * x, axis=-1, keepdims=True) + EPS) * g


def _params(sem):
    return pltpu.CompilerParams(dimension_semantics=sem, vmem_limit_bytes=VMEM_LIMIT)


def _full(shape):
    return pl.BlockSpec(shape, lambda *_: (0,) * len(shape))


def _proj_kernel(has_res, *refs):
    if has_res:
        x_ref, y_ref = refs[:2]
        refs = refs[2:]
    else:
        x_ref = refs[0]
        refs = refs[1:]
    (g1_ref, w1_ref, wab_ref, bab_ref, cbd_ref, sbd_ref, wmix_ref, qg_ref, wq1_ref, wq2_ref,
     cost_ref, sint_ref, kvg_ref, wk_ref, cosk_ref, sink_ref, wvt_ref) = refs[:17]
    outs = refs[17:]
    if has_res:
        xs_ref = outs[0]
        outs = outs[1:]
    (gq_ref, gk_ref, gv_ref, gg_ref, laf_ref, lab_ref, zr_ref, zi_ref, qt_ref, kk_ref, vt_ref) = outs

    x = x_ref[0]
    if has_res:
        x = x + y_ref[0]
        xs_ref[0] = x
    xn = _rms(x, g1_ref[...]).astype(BF16)
    p = _dot(xn, w1_ref[...])
    gq_ref[0] = (p[:, 0:128] * (GLA_DK ** -0.5)).astype(BF16)
    gk_ref[0] = p[:, 128:256].astype(BF16)
    gv_ref[0] = p[:, 256:512].astype(BF16)
    gg_ref[0] = p[:, 512:768].astype(BF16)
    pre = _dot(p[:, 768:896].astype(BF16), wab_ref[...]) + bab_ref[...]
    la = (jnp.minimum(pre, 0.0) - jnp.log(1.0 + jnp.exp(-jnp.abs(pre)))) * (1.0 / GLA_TAU)
    laf_ref[0] = la[:, 0:128]
    lab_ref[0] = la[:, 128:256]
    u = p[:, 896:1152].astype(BF16)
    zr_ref[0] = _dot(_dot(u, cbd_ref[...]).astype(BF16), wmix_ref[...]).astype(BF16)
    zi_ref[0] = (-_dot(_dot(u, sbd_ref[...]).astype(BF16), wmix_ref[...])).astype(BF16)
    cqn = _rms(p[:, 1152:1408], qg_ref[...]).astype(BF16)
    qa = lax.dot_general(wq1_ref[...], cqn, NT_DIMS, preferred_element_type=F32)
    qb = lax.dot_general(wq2_ref[...], cqn, NT_DIMS, preferred_element_type=F32)
    cos_t = cost_ref[0]
    sin_t = sint_ref[0]
    scale = (MLA_NOPE + MLA_ROPE) ** -0.5 * math.log2(math.e)
    for h in range(MLA_HEADS):
        sl = slice(h * MLA_QK_PAD, (h + 1) * MLA_QK_PAD)
        qt_ref[0, h] = ((qa[sl] * cos_t + qb[sl] * sin_t) * scale).astype(BF16)
    kvn = _rms(p[:, 1408:1536], kvg_ref[...]).astype(BF16)
    kf = _dot(kvn, wk_ref[...])
    krope = p[:, 1536:1664] * cosk_ref[0] + p[:, 1664:1792] * sink_ref[0]
    for h in range(MLA_HEADS):
        sl = slice(h * MLA_QK_PAD, (h + 1) * MLA_QK_PAD)
        kk_ref[0, h] = (kf[:, sl] + krope).astype(BF16)
    vt = lax.dot_general(wvt_ref[...], kvn, NT_DIMS, preferred_element_type=F32)
    tm = vt.shape[1]
    ones = jnp.ones((MLA_V_ROWS - MLA_V, tm), BF16)
    for h in range(MLA_HEADS):
        vt_ref[0, h, 0:MLA_V, :] = vt[h * MLA_V:(h + 1) * MLA_V].astype(BF16)
        vt_ref[0, h, MLA_V:MLA_V_ROWS, :] = ones


def _proj(x, y, lw, tabs, tm):
    B, S, D = x.shape
    has_res = y is not None
    H = MLA_HEADS
    tok = lambda w: pl.BlockSpec((1, tm, w), lambda b, i: (b, i, 0))
    in_specs = [tok(D)] + ([tok(D)] if has_res else [])
    weights = [lw["g1"], lw["w1"], lw["wab"], lw["bab"], tabs["cbd"], tabs["sbd"], lw["wmix"], lw["qg"],
               lw["wq1"], lw["wq2"]]
    in_specs += [_full(w.shape) for w in weights]
    in_specs += [pl.BlockSpec((1, MLA_QK_PAD, tm), lambda b, i: (b, 0, i))] * 2
    in_specs += [_full(lw["kvg"].shape), _full(lw["wk"].shape)]
    in_specs += [tok(MLA_QK_PAD)] * 2
    in_specs += [_full(lw["wvt"].shape)]
    args = [x] + ([y] if has_res else []) + weights + [tabs["cos_t"], tabs["sin_t"], lw["kvg"], lw["wk"],
                                                       tabs["cos_k"], tabs["sin_k"], lw["wvt"]]
    out_shape, out_specs = [], []
    if has_res:
        out_shape.append(jax.ShapeDtypeStruct((B, S, D), F32))
        out_specs.append(tok(D))
    for w, dt in ((128, BF16), (128, BF16), (256, BF16), (256, BF16), (128, F32), (128, F32), (256, BF16),
                  (256, BF16)):
        out_shape.append(jax.ShapeDtypeStruct((B, S, w), dt))
        out_specs.append(tok(w))
    out_shape.append(jax.ShapeDtypeStruct((B, H, MLA_QK_PAD, S), BF16))
    out_specs.append(pl.BlockSpec((1, H, MLA_QK_PAD, tm), lambda b, i: (b, 0, 0, i)))
    out_shape.append(jax.ShapeDtypeStruct((B, H, S, MLA_QK_PAD), BF16))
    out_specs.append(pl.BlockSpec((1, H, tm, MLA_QK_PAD), lambda b, i: (b, 0, i, 0)))
    out_shape.append(jax.ShapeDtypeStruct((B, H, MLA_V_ROWS, S), BF16))
    out_specs.append(pl.BlockSpec((1, H, MLA_V_ROWS, tm), lambda b, i: (b, 0, 0, i)))
    res = pl.pallas_call(
        functools.partial(_proj_kernel, has_res),
        grid=(B, S // tm), in_specs=in_specs, out_specs=out_specs, out_shape=out_shape,
        compiler_params=_params(("parallel", "parallel")), name="proj")(*args)
    res = list(res)
    xs = res.pop(0) if has_res else x
    return xs, res


def _gla_consts():
    C, W, QK = GLA_CHUNK, GLA_W, GLA_QK
    i64 = lax.broadcasted_iota(jnp.int32, (C, C), 0)
    j64 = lax.broadcasted_iota(jnp.int32, (C, C), 1)
    r_att = lax.broadcasted_iota(jnp.int32, (C, W), 0)
    c_att = lax.broadcasted_iota(jnp.int32, (C, W), 1) % C
    r_ks = lax.broadcasted_iota(jnp.int32, (W, QK), 0) // C
    c_ks = lax.broadcasted_iota(jnp.int32, (W, QK), 1) // GLA_DK
    r_bd = lax.broadcasted_iota(jnp.int32, (W, W), 0) // C
    c_bd = lax.broadcasted_iota(jnp.int32, (W, W), 1) // GLA_DV
    r_s = lax.broadcasted_iota(jnp.int32, (QK, W), 0) // GLA_DK
    c_s = lax.broadcasted_iota(jnp.int32, (QK, W), 1) // GLA_DV
    return dict(
        tri_f=(j64 <= i64).astype(F32), tri_b=(j64 >= i64).astype(F32),
        att_f=c_att <= r_att, att_b=c_att > r_att,
        ks=r_ks == c_ks, bd=r_bd == c_bd, st=r_s == c_s)


def _gla_chunk(q, k, v, la, state, fwd, cs):
    C = GLA_CHUNK
    b = _dot_hi(cs["tri_f"] if fwd else cs["tri_b"], la)
    b_edge = b[C - 1:C] if fwd else b[0:1]
    qd = (q * jnp.exp(b)).astype(BF16)
    ki = (k * jnp.exp(-b)).astype(BF16)
    ke = (k * jnp.exp(b_edge - b)).astype(BF16)
    kstack = jnp.where(cs["ks"], jnp.concatenate([ki] * GLA_HEADS, axis=0), jnp.zeros((), BF16))
    att = lax.dot_general(qd, kstack, NT_DIMS, preferred_element_type=F32)
    att = jnp.where(cs["att_f"] if fwd else cs["att_b"], att, 0.0).astype(BF16)
    vbd = jnp.where(cs["bd"], jnp.concatenate([v] * GLA_HEADS, axis=0), jnp.zeros((), BF16))
    o = _dot(att, vbd) + _dot(qd, state.astype(BF16))
    kv = lax.dot_general(ke, v, TN_DIMS, preferred_element_type=F32)
    dec = jnp.exp(jnp.broadcast_to(b_edge, (GLA_QK, GLA_QK)).T)
    state = jnp.concatenate([dec, dec], axis=1) * state + jnp.where(cs["st"], kv, 0.0)
    return o, state


def _gla_kernel(qf_ref, kf_ref, vf_ref, laf_ref, qb_ref, kb_ref, vb_ref, lab_ref, of_ref, ob_ref,
                sf_ref, sb_ref, *, nchunk):
    C = GLA_CHUNK

    @pl.when(pl.program_id(1) == 0)
    def _():
        sf_ref[...] = jnp.zeros_like(sf_ref)
        sb_ref[...] = jnp.zeros_like(sb_ref)

    cs = _gla_consts()

    def body(c, carry):
        s_f, s_b = carry
        rf = pl.ds(pl.multiple_of(c * C, C), C)
        o, s_f = _gla_chunk(qf_ref[0, rf, :].astype(F32), kf_ref[0, rf, :].astype(F32), vf_ref[0, rf, :],
                            laf_ref[0, rf, :], s_f, True, cs)
        of_ref[0, rf, :] = o
        rb = pl.ds(pl.multiple_of((nchunk - 1 - c) * C, C), C)
        o, s_b = _gla_chunk(qb_ref[0, rb, :].astype(F32), kb_ref[0, rb, :].astype(F32), vb_ref[0, rb, :],
                            lab_ref[0, rb, :], s_b, False, cs)
        ob_ref[0, rb, :] = o
        return s_f, s_b

    s_f, s_b = lax.fori_loop(0, nchunk, body, (sf_ref[...], sb_ref[...]))
    sf_ref[...] = s_f
    sb_ref[...] = s_b


def _gla(gq, gk, gv, la_f, la_b, tg):
    B, S, _ = gq.shape
    nb = S // tg
    fw = lambda w: pl.BlockSpec((1, tg, w), lambda b, i: (b, i, 0))
    bw = lambda w: pl.BlockSpec((1, tg, w), lambda b, i: (b, nb - 1 - i, 0))
    return pl.pallas_call(
        functools.partial(_gla_kernel, nchunk=tg // GLA_CHUNK),
        grid=(B, nb),
        in_specs=[fw(128), fw(128), fw(256), fw(128), bw(128), bw(128), bw(256), bw(128)],
        out_specs=[fw(256), bw(256)],
        out_shape=[jax.ShapeDtypeStruct((B, S, GLA_W), F32)] * 2,
        scratch_shapes=[pltpu.VMEM((GLA_QK, GLA_W), F32)] * 2,
        compiler_params=_params(("parallel", "arbitrary")), name="gla")(gq, gk, gv, la_f, gq, gk, gv, la_b)


def _fnet1_kernel(zr_ref, zi_ref, m1_ref, tc_ref, ts_ref, or_ref, oi_ref, *, ts1):
    n2 = zr_ref.shape[1]
    x = jnp.concatenate([zr_ref[0], zi_ref[0]], axis=0)
    z = _dot(m1_ref[...], x)
    zr, zi = z[:n2], z[n2:]
    tc, tsn = tc_ref[...], ts_ref[...]
    pr = zr * tc + zi * tsn
    pi = zi * tc - zr * tsn
    for j in range(ts1):
        sl = slice(j * FNET_W, (j + 1) * FNET_W)
        or_ref[0, j] = pr[:, sl].astype(BF16)
        oi_ref[0, j] = pi[:, sl].astype(BF16)


def _fnet2_kernel(zr_ref, zi_ref, m2_ref, o_ref):
    x = jnp.concatenate([zr_ref[0], zi_ref[0]], axis=0)
    o_ref[0] = _dot(m2_ref[...], x).astype(BF16)


def _fnet(zr, zi, tabs, ts1, tk2):
    B, S, CH = zr.shape
    n1, n2 = tabs["n1"], tabs["n2"]
    zr2 = zr.reshape(B, n2, n1 * CH)
    zi2 = zi.reshape(B, n2, n1 * CH)
    cols = ts1 * CH
    in_b = pl.BlockSpec((1, n2, cols), lambda b, i: (b, 0, i))
    tw_b = pl.BlockSpec((n2, cols), lambda b, i: (0, i))
    out_b = pl.BlockSpec((1, ts1, n2, CH), lambda b, i: (b, i, 0, 0))
    pr, pi = pl.pallas_call(
        functools.partial(_fnet1_kernel, ts1=ts1),
        grid=(B, n1 // ts1),
        in_specs=[in_b, in_b, _full(tabs["m1"].shape), tw_b, tw_b],
        out_specs=[out_b, out_b],
        out_shape=[jax.ShapeDtypeStruct((B, n1, n2, CH), BF16)] * 2,
        compiler_params=_params(("parallel", "parallel")), name="fnet1")(zr2, zi2, tabs["m1"], tabs["tw_c"],
                                                                         tabs["tw_s"])
    pr = pr.reshape(B, n1, n2 * CH)
    pi = pi.reshape(B, n1, n2 * CH)
    cols2 = tk2 * CH
    blk = pl.BlockSpec((1, n1, cols2), lambda b, i: (b, 0, i))
    out = pl.pallas_call(
        _fnet2_kernel, grid=(B, n2 // tk2),
        in_specs=[blk, blk, _full(tabs["m2"].shape)], out_specs=blk,
        out_shape=jax.ShapeDtypeStruct((B, n1, n2 * CH), BF16),
        compiler_params=_params(("parallel", "parallel")), name="fnet2")(pr, pi, tabs["m2"])
    return out.reshape(B, S, CH)


def _attn_kernel(qt_ref, k_ref, vt_ref, o_ref, s0_ref, s1_ref, *, tk, nk):
    qt = qt_ref[0, 0]
    tq = qt.shape[1]

    def scores(j, s_ref):
        ks = pl.ds(pl.multiple_of(j * tk, tk), tk)
        s = _dot(k_ref[0, 0, ks, :], qt)
        s_ref[...] = s
        return jnp.max(s, axis=0, keepdims=True)

    def update(j, s_ref, mt, m, acc):
        ks = pl.ds(pl.multiple_of(j * tk, tk), tk)
        m_new = jnp.maximum(m, mt)
        p = jnp.exp2((s_ref[...] - m_new).astype(BF16))
        acc = jnp.exp2(m - m_new) * acc + _dot(vt_ref[0, 0, :, ks], p)
        return m_new, acc

    def body(jj, carry):
        m, acc, mt0 = carry
        j = 2 * jj
        mt1 = scores(j + 1, s1_ref)
        m, acc = update(j, s0_ref, mt0, m, acc)
        mt0 = scores(j + 2, s0_ref)
        m, acc = update(j + 1, s1_ref, mt1, m, acc)
        return m, acc, mt0

    m0 = jnp.full((1, tq), -1e30, F32)
    acc0 = jnp.zeros((MLA_V_ROWS, tq), F32)
    m, acc, mt0 = lax.fori_loop(0, nk // 2 - 1, body, (m0, acc0, scores(0, s0_ref)))
    mt1 = scores(nk - 1, s1_ref)
    m, acc = update(nk - 2, s0_ref, mt0, m, acc)
    m, acc = update(nk - 1, s1_ref, mt1, m, acc)
    o_ref[0] = (acc[0:MLA_V] / acc[MLA_V:MLA_V + 1]).astype(BF16)


def _attn(qt, kk, vt, tq, tk):
    B, H, _, S = qt.shape
    assert (S // tk) % 2 == 0
    return pl.pallas_call(
        functools.partial(_attn_kernel, tk=tk, nk=S // tk),
        grid=(B, H, S // tq),
        in_specs=[pl.BlockSpec((1, 1, MLA_QK_PAD, tq), lambda b, h, i: (b, h, 0, i)),
                  pl.BlockSpec((1, 1, S, MLA_QK_PAD), lambda b, h, i: (b, h, 0, 0)),
                  pl.BlockSpec((1, 1, MLA_V_ROWS, S), lambda b, h, i: (b, h, 0, 0))],
        out_specs=pl.BlockSpec((1, MLA_V, tq), lambda b, h, i: (b, h, i)),
        out_shape=jax.ShapeDtypeStruct((B, H * MLA_V, S), BF16),
        scratch_shapes=[pltpu.VMEM((tk, tq), F32)] * 2,
        compiler_params=_params(("parallel", "parallel", "arbitrary")), name="attn")(qt, kk, vt)


def _outproj_kernel(x_ref, of_ref, ob_ref, gg_ref, gng_ref, hm_ref, yf_ref, ot_ref, wo_ref, n2g_ref,
                    wrh_ref, wrl_ref, br_ref, ut_ref, lt_ref,
                    x1_ref, h_ref, info_ref, off_ref, cnt_ref):
    o = of_ref[0] + ob_ref[0]
    msq = _dot((o * o).astype(BF16), hm_ref[...])
    gate = gg_ref[0].astype(F32)
    og = o * lax.rsqrt(msq + EPS) * gng_ref[...] * (gate * jax.nn.sigmoid(gate))
    acc = _dot(og.astype(BF16), wo_ref[0:GLA_W, :])
    acc += _dot(yf_ref[0], wo_ref[GLA_W:GLA_W + FNET_W, :])
    acc += lax.dot_general(ot_ref[0], wo_ref[GLA_W + FNET_W:, :], TN_DIMS, preferred_element_type=F32)
    x1 = x_ref[0] + acc
    x1_ref[0] = x1
    h = _rms(x1, n2g_ref[...])
    h_hi = h.astype(BF16)
    h_ref[0] = h_hi
    h_lo = (h - h_hi.astype(F32)).astype(BF16)
    nt = lambda w, a: lax.dot_general(w, a, NT_DIMS, preferred_element_type=F32)
    lg = nt(wrh_ref[...], h_hi) + nt(wrh_ref[...], h_lo) + nt(wrl_ref[...], h_hi) + br_ref[...]
    tb = lg.shape[1]
    neg = jnp.float32(-jnp.inf)
    row8 = lax.broadcasted_iota(jnp.int32, (8, tb), 0)
    g8 = lg[128:136]
    gm = jnp.where(row8 < N_GROUPS, g8, neg)
    gmax = jnp.max(gm, axis=0, keepdims=True)
    gidx = jnp.min(jnp.where(gm == gmax, row8, 99), axis=0, keepdims=True)
    gsum = jnp.sum(jnp.where(row8 < N_GROUPS, jnp.exp(g8 - gmax), 0.0), axis=0, keepdims=True)
    grp_w = 1.0 / gsum
    row = lax.broadcasted_iota(jnp.int32, (N_EXPERTS, tb), 0)
    em = jnp.where((row // EXPERTS_PER_GROUP) == gidx, lg[0:N_EXPERTS], neg)
    e1 = jnp.max(em, axis=0, keepdims=True)
    i1 = jnp.min(jnp.where(em == e1, row, 99), axis=0, keepdims=True)
    em2 = jnp.where(row == i1, neg, em)
    e2 = jnp.max(em2, axis=0, keepdims=True)
    i2 = jnp.min(jnp.where(em2 == e2, row, 99), axis=0, keepdims=True)
    r = jnp.exp(e2 - e1)
    c1 = grp_w / (1.0 + r)
    c2 = grp_w * r / (1.0 + r)
    sel1 = row == i1
    sel2 = row == i2
    oh = jnp.where(sel1 | sel2, 1.0, 0.0)
    rank = _dot(oh.astype(BF16), ut_ref[...])
    cnt = jnp.sum(oh, axis=1, keepdims=True)
    padded = jnp.ceil(cnt * (1.0 / SEG_ALIGN)) * SEG_ALIGN
    off = _dot_hi(lt_ref[...], jnp.broadcast_to(padded, (N_EXPERTS, 128)))
    pos = off[:, 0:1] + rank
    pos1 = jnp.sum(jnp.where(sel1, pos, 0.0), axis=0, keepdims=True)
    pos2 = jnp.sum(jnp.where(sel2, pos, 0.0), axis=0, keepdims=True)
    info_ref[0] = jnp.concatenate([pos1, pos2, c1, c2, jnp.zeros((4, tb), F32)], axis=0)
    off_ref[0] = off.astype(jnp.int32)
    cnt_ref[0] = jnp.broadcast_to(cnt, (N_EXPERTS, 128)).astype(jnp.int32)


def _outproj(x, o_f, o_b, gg, yf, ot, lw, tabs, tb):
    B, S, D = x.shape
    nsb = S // tb
    tok = lambda w: pl.BlockSpec((1, tb, w), lambda b, i: (b, i, 0))
    weights_a = [lw["gng"], tabs["hm"]]
    weights_b = [lw["wo"], lw["n2g"], lw["wrh"], lw["wrl"], lw["br"], tabs["ut"], tabs["lt"]]
    in_specs = ([tok(D), tok(GLA_W), tok(GLA_W), tok(GLA_W)] + [_full(w.shape) for w in weights_a]
                + [tok(FNET_W), pl.BlockSpec((1, MLA_W, tb), lambda b, i: (b, 0, i))]
                + [_full(w.shape) for w in weights_b])
    blk = lambda r, c, dt: (jax.ShapeDtypeStruct((B * nsb, r, c), dt),
                            pl.BlockSpec((1, r, c), lambda b, i: (b * nsb + i, 0, 0)))
    outs = [(jax.ShapeDtypeStruct((B, S, D), F32), tok(D)), (jax.ShapeDtypeStruct((B, S, D), BF16), tok(D)),
            blk(8, tb, F32), blk(N_EXPERTS, 128, jnp.int32), blk(N_EXPERTS, 128, jnp.int32)]
    return pl.pallas_call(
        _outproj_kernel, grid=(B, nsb), in_specs=in_specs,
        out_specs=[o[1] for o in outs], out_shape=[o[0] for o in outs],
        compiler_params=_params(("parallel", "parallel")), name="outproj")(
            x, o_f, o_b, gg, *weights_a, yf, ot, *weights_b)


def _moe_kernel(off_ref, cnt_ref, h_ref, info_ref, wg_ref, wu_ref, wd_ref, y_ref,
                pm_ref, xs_ref, ow_ref, cs_ref):
    blk = pl.program_id(0)
    e = pl.program_id(1)
    ns, tb = pm_ref.shape

    @pl.when(e == 0)
    def _():
        info = info_ref[0]
        pos1 = info[0:1].astype(jnp.int32)
        pos2 = info[1:2].astype(jnp.int32)
        slot = lax.broadcasted_iota(jnp.int32, (ns, tb), 0)
        p1 = slot == pos1
        p2 = slot == pos2
        pm = jnp.where(p1 | p2, 1.0, 0.0).astype(BF16)
        pm_ref[...] = pm
        cs_ref[...] = jnp.sum(jnp.where(p1, info[2:3], 0.0) + jnp.where(p2, info[3:4], 0.0), axis=1,
                              keepdims=True)
        xs_ref[...] = _dot(pm, h_ref[...]).astype(BF16)
        ow_ref[...] = jnp.zeros_like(ow_ref)

    off_e = off_ref[blk * N_EXPERTS + e]
    ntile = (cnt_ref[blk * N_EXPERTS + e] + ROW_TILE - 1) // ROW_TILE

    def tile(i, carry):
        rows = pl.ds(pl.multiple_of(off_e + i * ROW_TILE, SEG_ALIGN), ROW_TILE)
        xt = xs_ref[rows, :]
        g = _dot(xt, wg_ref[0])
        u = _dot(xt, wu_ref[0])
        hh = (g * jax.nn.sigmoid(g) * u).astype(BF16)
        o = _dot(hh, wd_ref[0])
        ow_ref[rows, :] = (o * cs_ref[rows, :]).astype(BF16)
        return carry

    lax.fori_loop(0, ntile, tile, 0)

    @pl.when(e == N_EXPERTS - 1)
    def _():
        y_ref[...] = lax.dot_general(pm_ref[...], ow_ref[...], TN_DIMS, preferred_element_type=F32)


def _moe(h, info, off, cnt, lw, tb):
    T, D = h.shape
    nblk = T // tb
    ns = 2 * tb + N_EXPERTS * SEG_ALIGN + ROW_TILE
    grid_spec = pltpu.PrefetchScalarGridSpec(
        num_scalar_prefetch=2, grid=(nblk, N_EXPERTS),
        in_specs=[pl.BlockSpec((tb, D), lambda b, e, *_: (b, 0)),
                  pl.BlockSpec((1, 8, tb), lambda b, e, *_: (b, 0, 0)),
                  pl.BlockSpec((1, D, D_EXPERT), lambda b, e, *_: (e, 0, 0)),
                  pl.BlockSpec((1, D, D_EXPERT), lambda b, e, *_: (e, 0, 0)),
                  pl.BlockSpec((1, D_EXPERT, D), lambda b, e, *_: (e, 0, 0))],
        out_specs=pl.BlockSpec((tb, D), lambda b, e, *_: (b, 0)),
        scratch_shapes=[pltpu.VMEM((ns, tb), BF16), pltpu.VMEM((ns, D), BF16), pltpu.VMEM((ns, D), BF16),
                        pltpu.VMEM((ns, 1), F32)])
    return pl.pallas_call(
        _moe_kernel, grid_spec=grid_spec, out_shape=jax.ShapeDtypeStruct((T, D), F32),
        compiler_params=_params(("parallel", "arbitrary")), name="moe")(
            off, cnt, h, info, lw["wg"], lw["wu"], lw["wd"])


def _final_kernel(x_ref, y_ref, g_ref, o_ref):
    o_ref[...] = _rms(x_ref[...] + y_ref[...], g_ref[...])


def _final(x, y, g, tm):
    T, D = x.shape
    blk = pl.BlockSpec((tm, D), lambda i: (i, 0))
    return pl.pallas_call(
        _final_kernel, grid=(T // tm,), in_specs=[blk, blk, _full((1, D))], out_specs=blk,
        out_shape=jax.ShapeDtypeStruct((T, D), F32),
        compiler_params=_params(("parallel",)), name="final")(x, y, g)


def _block_diag(blocks):
    n = len(blocks)
    r, c = blocks[0].shape
    out = jnp.zeros((n * r, n * c), blocks[0].dtype)
    for i, b in enumerate(blocks):
        out = out.at[i * r:(i + 1) * r, i * c:(i + 1) * c].set(b)
    return out


def _tables(positions, S, tb):
    B = positions.shape[0]
    n1 = 1 << ((S.bit_length() - 1 + 1) // 2)
    n2 = S // n1
    assert n1 * n2 == S
    half = MLA_ROPE // 2
    inv = 1.0 / (ROPE_THETA ** (jnp.arange(0, MLA_ROPE, 2, dtype=F32) / MLA_ROPE))
    ang = positions.astype(F32)[..., None] * inv
    cos, sin = jnp.cos(ang), jnp.sin(ang)
    z = lambda n: jnp.zeros((B, S, n), F32)
    pad = MLA_QK_PAD - MLA_NOPE - MLA_ROPE
    cos_q = jnp.concatenate([jnp.ones((B, S, MLA_NOPE), F32), cos, cos, z(pad)], -1)
    sin_q = jnp.concatenate([z(MLA_NOPE), -sin, sin, z(pad)], -1)
    cos_k = jnp.concatenate([z(MLA_NOPE), cos, cos, z(pad)], -1)
    dft = lambda n: 2.0 * np.pi * np.outer(np.arange(n), np.arange(n)) / n
    a64 = dft(FNET_DG)
    c64 = jnp.asarray(np.cos(a64) / math.sqrt(FNET_DG), F32)
    s64 = jnp.asarray(np.sin(a64) / math.sqrt(FNET_DG), F32)
    a2, a1 = dft(n2), dft(n1)
    c2, s2 = np.cos(a2) / math.sqrt(n2), np.sin(a2) / math.sqrt(n2)
    c1, s1 = np.cos(a1) / math.sqrt(n1), np.sin(a1) / math.sqrt(n1)
    m1 = np.block([[c2, s2], [-s2, c2]])
    m2 = np.concatenate([c1, s1], axis=1)
    kk = jnp.arange(n2, dtype=jnp.int32)[:, None] * jnp.arange(n1, dtype=jnp.int32)[None, :]
    th = (kk % S).astype(F32) * (2.0 * math.pi / S)
    hm = np.kron(np.eye(GLA_HEADS), np.full((GLA_DV, GLA_DV), 1.0 / GLA_DV))
    tt = np.arange(tb)
    return dict(
        n1=n1, n2=n2,
        cos_t=cos_q.transpose(0, 2, 1), sin_t=sin_q.transpose(0, 2, 1), cos_k=cos_k, sin_k=sin_q,
        cbd=_block_diag([c64] * FNET_GROUPS).astype(BF16), sbd=_block_diag([s64] * FNET_GROUPS).astype(BF16),
        m1=jnp.asarray(m1, BF16), m2=jnp.asarray(m2, BF16),
        tw_c=jnp.repeat(jnp.cos(th), FNET_W, axis=1), tw_s=jnp.repeat(jnp.sin(th), FNET_W, axis=1),
        hm=jnp.asarray(hm, BF16),
        ut=jnp.asarray(tt[:, None] < tt[None, :], BF16),
        lt=jnp.asarray(np.arange(N_EXPERTS)[None, :] < np.arange(N_EXPERTS)[:, None], F32))


def _layer_weights(l, p):
    w_in = p["w_in"][l]
    D = w_in.shape[0]
    o = np.cumsum([0, GLA_QK, GLA_QK, GLA_W, GLA_W, GLA_RANK, GLA_RANK, FNET_W, MLA_Q_RANK, MLA_KV_RANK, MLA_ROPE])
    col = lambda i: w_in[:, int(o[i]):int(o[i + 1])]
    zc = lambda n: jnp.zeros((D, n), F32)
    half = MLA_ROPE // 2
    kr = col(9)
    pad = MLA_QK_PAD - MLA_NOPE - MLA_ROPE
    kr1 = jnp.concatenate([zc(MLA_NOPE), kr[:, :half], kr[:, half:], zc(pad)], 1)
    kr2 = jnp.concatenate([zc(MLA_NOPE), kr[:, half:], kr[:, :half], zc(pad)], 1)
    w1 = jnp.concatenate([col(0), col(1), col(2), col(3), col(4), col(5), zc(128 - 2 * GLA_RANK), col(6), col(7),
                          col(8), kr1, kr2], 1).astype(BF16)
    wab = jnp.zeros((128, 2 * GLA_QK), F32)
    wab = wab.at[0:GLA_RANK, 0:GLA_QK].set(p["gla_wa_f"][l])
    wab = wab.at[GLA_RANK:2 * GLA_RANK, GLA_QK:].set(p["gla_wa_b"][l])
    bab = jnp.concatenate([p["gla_ba_f"][l], p["gla_ba_b"][l]])[None, :]
    wuq = p["mla_w_uq"][l].reshape(MLA_Q_RANK, MLA_HEADS, MLA_NOPE + MLA_ROPE)
    zq = lambda n: jnp.zeros((MLA_Q_RANK, MLA_HEADS, n), F32)
    nope, x1, x2 = wuq[..., :MLA_NOPE], wuq[..., MLA_NOPE:MLA_NOPE + half], wuq[..., MLA_NOPE + half:]
    wq1 = jnp.concatenate([nope, x1, x2, zq(pad)], -1).reshape(MLA_Q_RANK, -1)
    wq2 = jnp.concatenate([zq(MLA_NOPE), x2, x1, zq(pad)], -1).reshape(MLA_Q_RANK, -1)
    wukv = p["mla_w_ukv"][l].reshape(MLA_KV_RANK, MLA_HEADS, MLA_NOPE + MLA_V)
    wk = jnp.concatenate([wukv[..., :MLA_NOPE], jnp.zeros((MLA_KV_RANK, MLA_HEADS, MLA_QK_PAD - MLA_NOPE), F32)],
                         -1).reshape(MLA_KV_RANK, -1)
    wvt = wukv[..., MLA_NOPE:].reshape(MLA_KV_RANK, -1).T
    wr = jnp.zeros((256, D), F32)
    wr = wr.at[0:N_EXPERTS].set(p["moe_w_re"][l].T).at[128:128 + N_GROUPS].set(p["moe_w_rg"][l].T)
    br = jnp.zeros((256, 1), F32)
    br = br.at[0:N_EXPERTS, 0].set(p["moe_b_re"][l]).at[128:128 + N_GROUPS, 0].set(p["moe_b_rg"][l])
    wrh = wr.astype(BF16)
    return dict(
        g1=p["norm1_g"][l][None, :], w1=w1, wab=wab.astype(BF16), bab=bab,
        wmix=_block_diag([p["fnet_w"][l, g] for g in range(FNET_GROUPS)]).astype(BF16),
        qg=p["mla_q_norm_g"][l][None, :], wq1=wq1.T.astype(BF16), wq2=wq2.T.astype(BF16),
        kvg=p["mla_kv_norm_g"][l][None, :], wk=wk.astype(BF16), wvt=wvt.astype(BF16),
        gng=jnp.tile(p["gla_norm_g"][l], GLA_HEADS)[None, :], wo=p["w_out"][l].astype(BF16),
        n2g=p["norm2_g"][l][None, :], wrh=wrh, wrl=(wr - wrh.astype(F32)).astype(BF16), br=br,
        wg=p["moe_w_gate"][l].astype(BF16), wu=p["moe_w_up"][l].astype(BF16), wd=p["moe_w_down"][l].astype(BF16))


def _tiles(S):
    return dict(tm=min(512, S), tg=min(512, S), ts1=8, tk2=8, tq=min(512, S), tk=min(1024, S // 2), tb=min(1024, S))


def kernel(x, positions, norm1_g, w_in, gla_wa_f, gla_ba_f, gla_wa_b, gla_ba_b, gla_norm_g, fnet_w, mla_q_norm_g,
           mla_w_uq, mla_kv_norm_g, mla_w_ukv, w_out, norm2_g, moe_w_rg, moe_b_rg, moe_w_re, moe_b_re, moe_w_gate,
           moe_w_up, moe_w_down, final_norm_g):
    p = dict(norm1_g=norm1_g, w_in=w_in, gla_wa_f=gla_wa_f, gla_ba_f=gla_ba_f, gla_wa_b=gla_wa_b,
             gla_ba_b=gla_ba_b, gla_norm_g=gla_norm_g, fnet_w=fnet_w, mla_q_norm_g=mla_q_norm_g,
             mla_w_uq=mla_w_uq, mla_kv_norm_g=mla_kv_norm_g, mla_w_ukv=mla_w_ukv, w_out=w_out, norm2_g=norm2_g,
             moe_w_rg=moe_w_rg, moe_b_rg=moe_b_rg, moe_w_re=moe_w_re, moe_b_re=moe_b_re, moe_w_gate=moe_w_gate,
             moe_w_up=moe_w_up, moe_w_down=moe_w_down)
    B, S, D = x.shape
    t = _tiles(S)
    tabs = _tables(positions, S, t["tb"])
    y = None
    for l in range(w_in.shape[0]):
        lw = _layer_weights(l, p)
        x, (gq, gk, gv, gg, la_f, la_b, zr, zi, qt, kk, vt) = _proj(x, y, lw, tabs, t["tm"])
        o_f, o_b = _gla(gq, gk, gv, la_f, la_b, t["tg"])
        yf = _fnet(zr, zi, tabs, min(t["ts1"], tabs["n1"]), min(t["tk2"], tabs["n2"]))
        ot = _attn(qt, kk, vt, t["tq"], t["tk"])
        x, h, info, off, cnt = _outproj(x, o_f, o_b, gg, yf, ot, lw, tabs, t["tb"])
        y = _moe(h.reshape(B * S, D), info, off[:, :, 0].reshape(-1), cnt[:, :, 0].reshape(-1), lw, t["tb"])
        y = y.reshape(B, S, D)
    return _final(x.reshape(B * S, D), y.reshape(B * S, D), final_norm_g[None, :], t["tm"]).reshape(B, S, D)
```

```python
import functools
import math

import numpy as np
import jax
import jax.numpy as jnp
from jax import lax
from jax.experimental import pallas as pl
from jax.experimental.pallas import tpu as pltpu

F32 = jnp.float32
BF16 = jnp.bfloat16
EPS = 1e-6

GLA_HEADS, GLA_DK, GLA_DV, GLA_QK, GLA_W, GLA_RANK = 4, 32, 64, 128, 256, 16
GLA_TAU, GLA_CHUNK = 16.0, 64
FNET_GROUPS, FNET_DG, FNET_W = 4, 64, 256
MLA_HEADS, MLA_NOPE, MLA_ROPE, MLA_V = 8, 64, 32, 64
MLA_Q_RANK, MLA_KV_RANK, MLA_W = 256, 128, 512
MLA_QK_PAD = 128
MLA_V_ROWS = 80
ATTN_CHUNK = 256
ROPE_THETA = 10000.0
N_GROUPS, EXPERTS_PER_GROUP, N_EXPERTS, D_EXPERT = 4, 8, 32, 256

SEG_ALIGN = 16
ROW_TILE = 128
VMEM_LIMIT = 56 * 1024 * 1024

NT_DIMS = (((1,), (1,)), ((), ()))
TN_DIMS = (((0,), (0,)), ((), ()))


def _dot(a, b):
    return jnp.dot(a, b, preferred_element_type=F32)


def _dot_hi(a, b):
    return jnp.dot(a, b, preferred_element_type=F32, precision=lax.Precision.HIGHEST)


def _rms(x, g):
    return x * lax.rsqrt(jnp.mean(x * x, axis=-1, keepdims=True) + EPS) * g


def _params(sem):
    return pltpu.CompilerParams(dimension_semantics=sem, vmem_limit_bytes=VMEM_LIMIT)


def _full(shape):
    return pl.BlockSpec(shape, lambda *_: (0,) * len(shape))


def _proj_kernel(has_res, *refs):
    if has_res:
        x_ref, y_ref = refs[:2]
        refs = refs[2:]
    else:
        x_ref = refs[0]
        refs = refs[1:]
    (g1_ref, w1_ref, wab_ref, bab_ref, cbd_ref, sbd_ref, wmix_ref, qg_ref, wq1_ref, wq2_ref,
     cost_ref, sint_ref, kvg_ref, wk_ref, cosk_ref, sink_ref, wvt_ref) = refs[:17]
    outs = refs[17:]
    if has_res:
        xs_ref = outs[0]
        outs = outs[1:]
    (gq_ref, gk_ref, gv_ref, gg_ref, laf_ref, lab_ref, zr_ref, zi_ref, qt_ref, kk_ref, vt_ref) = outs

    x = x_ref[0]
    if has_res:
        x = x + y_ref[0]
        xs_ref[0] = x
    xn = _rms(x, g1_ref[...]).astype(BF16)
    p = _dot(xn, w1_ref[...])
    gq_ref[0] = (p[:, 0:128] * (GLA_DK ** -0.5)).astype(BF16)
    gk_ref[0] = p[:, 128:256].astype(BF16)
    gv_ref[0] = p[:, 256:512].astype(BF16)
    gg_ref[0] = p[:, 512:768].astype(BF16)
    pre = _dot(p[:, 768:896].astype(BF16), wab_ref[...]) + bab_ref[...]
    la = (jnp.minimum(pre, 0.0) - jnp.log(1.0 + jnp.exp(-jnp.abs(pre)))) * (1.0 / GLA_TAU)
    laf_ref[0] = la[:, 0:128]
    lab_ref[0] = la[:, 128:256]
    u = p[:, 896:1152].astype(BF16)
    zr_ref[0] = _dot(_dot(u, cbd_ref[...]).astype(BF16), wmix_ref[...]).astype(BF16)
    zi_ref[0] = (-_dot(_dot(u, sbd_ref[...]).astype(BF16), wmix_ref[...])).astype(BF16)
    cqn = _rms(p[:, 1152:1408], qg_ref[...]).astype(BF16)
    qa = lax.dot_general(wq1_ref[...], cqn, NT_DIMS, preferred_element_type=F32)
    qb = lax.dot_general(wq2_ref[...], cqn, NT_DIMS, preferred_element_type=F32)
    cos_t = cost_ref[0]
    sin_t = sint_ref[0]
    scale = (MLA_NOPE + MLA_ROPE) ** -0.5 * math.log2(math.e)
    for h in range(MLA_HEADS):
        sl = slice(h * MLA_QK_PAD, (h + 1) * MLA_QK_PAD)
        qt_ref[0, h] = ((qa[sl] * cos_t + qb[sl] * sin_t) * scale).astype(BF16)
    kvn = _rms(p[:, 1408:1536], kvg_ref[...]).astype(BF16)
    kf = _dot(kvn, wk_ref[...])
    krope = p[:, 1536:1664] * cosk_ref[0] + p[:, 1664:1792] * sink_ref[0]
    for h in range(MLA_HEADS):
        sl = slice(h * MLA_QK_PAD, (h + 1) * MLA_QK_PAD)
        kk_ref[0, h] = (kf[:, sl] + krope).astype(BF16)
    vt = lax.dot_general(wvt_ref[...], kvn, NT_DIMS, preferred_element_type=F32)
    tm = vt.shape[1]
    ones = jnp.ones((MLA_V_ROWS - MLA_V, tm), BF16)
    for h in range(MLA_HEADS):
        vt_ref[0, h, 0:MLA_V, :] = vt[h * MLA_V:(h + 1) * MLA_V].astype(BF16)
        vt_ref[0, h, MLA_V:MLA_V_ROWS, :] = ones


def _proj(x, y, lw, tabs, tm):
    B, S, D = x.shape
    has_res = y is not None
    H = MLA_HEADS
    tok = lambda w: pl.BlockSpec((1, tm, w), lambda b, i: (b, i, 0))
    in_specs = [tok(D)] + ([tok(D)] if has_res else [])
    weights = [lw["g1"], lw["w1"], lw["wab"], lw["bab"], tabs["cbd"], tabs["sbd"], lw["wmix"], lw["qg"],
               lw["wq1"], lw["wq2"]]
    in_specs += [_full(w.shape) for w in weights]
    in_specs += [pl.BlockSpec((1, MLA_QK_PAD, tm), lambda b, i: (b, 0, i))] * 2
    in_specs += [_full(lw["kvg"].shape), _full(lw["wk"].shape)]
    in_specs += [tok(MLA_QK_PAD)] * 2
    in_specs += [_full(lw["wvt"].shape)]
    args = [x] + ([y] if has_res else []) + weights + [tabs["cos_t"], tabs["sin_t"], lw["kvg"], lw["wk"],
                                                       tabs["cos_k"], tabs["sin_k"], lw["wvt"]]
    out_shape, out_specs = [], []
    if has_res:
        out_shape.append(jax.ShapeDtypeStruct((B, S, D), F32))
        out_specs.append(tok(D))
    for w, dt in ((128, BF16), (128, BF16), (256, BF16), (256, BF16), (128, F32), (128, F32), (256, BF16),
                  (256, BF16)):
        out_shape.append(jax.ShapeDtypeStruct((B, S, w), dt))
        out_specs.append(tok(w))
    out_shape.append(jax.ShapeDtypeStruct((B, H, MLA_QK_PAD, S), BF16))
    out_specs.append(pl.BlockSpec((1, H, MLA_QK_PAD, tm), lambda b, i: (b, 0, 0, i)))
    out_shape.append(jax.ShapeDtypeStruct((B, H, S, MLA_QK_PAD), BF16))
    out_specs.append(pl.BlockSpec((1, H, tm, MLA_QK_PAD), lambda b, i: (b, 0, i, 0)))
    out_shape.append(jax.ShapeDtypeStruct((B, H, MLA_V_ROWS, S), BF16))
    out_specs.append(pl.BlockSpec((1, H, MLA_V_ROWS, tm), lambda b, i: (b, 0, 0, i)))
    res = pl.pallas_call(
        functools.partial(_proj_kernel, has_res),
        grid=(B, S // tm), in_specs=in_specs, out_specs=out_specs, out_shape=out_shape,
        compiler_params=_params(("parallel", "parallel")), name="proj")(*args)
    res = list(res)
    xs = res.pop(0) if has_res else x
    return xs, res


def _gla_consts():
    C, W, QK = GLA_CHUNK, GLA_W, GLA_QK
    i64 = lax.broadcasted_iota(jnp.int32, (C, C), 0)
    j64 = lax.broadcasted_iota(jnp.int32, (C, C), 1)
    r_att = lax.broadcasted_iota(jnp.int32, (C, W), 0)
    c_att = lax.broadcasted_iota(jnp.int32, (C, W), 1) % C
    r_ks = lax.broadcasted_iota(jnp.int32, (W, QK), 0) // C
    c_ks = lax.broadcasted_iota(jnp.int32, (W, QK), 1) // GLA_DK
    r_bd = lax.broadcasted_iota(jnp.int32, (W, W), 0) // C
    c_bd = lax.broadcasted_iota(jnp.int32, (W, W), 1) // GLA_DV
    r_s = lax.broadcasted_iota(jnp.int32, (QK, W), 0) // GLA_DK
    c_s = lax.broadcasted_iota(jnp.int32, (QK, W), 1) // GLA_DV
    return dict(
        tri_f=(j64 <= i64).astype(F32), tri_b=(j64 >= i64).astype(F32),
        att_f=c_att <= r_att, att_b=c_att > r_att,
        ks=r_ks == c_ks, bd=r_bd == c_bd, st=r_s == c_s)


def _gla_chunk(q, k, v, la, state, fwd, cs):
    C = GLA_CHUNK
    b = _dot_hi(cs["tri_f"] if fwd else cs["tri_b"], la)
    b_edge = b[C - 1:C] if fwd else b[0:1]
    qd = (q * jnp.exp(b)).astype(BF16)
    ki = (k * jnp.exp(-b)).astype(BF16)
    ke = (k * jnp.exp(b_edge - b)).astype(BF16)
    kstack = jnp.where(cs["ks"], jnp.concatenate([ki] * GLA_HEADS, axis=0), jnp.zeros((), BF16))
    att = lax.dot_general(qd, kstack, NT_DIMS, preferred_element_type=F32)
    att = jnp.where(cs["att_f"] if fwd else cs["att_b"], att, 0.0).astype(BF16)
    vbd = jnp.where(cs["bd"], jnp.concatenate([v] * GLA_HEADS, axis=0), jnp.zeros((), BF16))
    o = _dot(att, vbd) + _dot(qd, state.astype(BF16))
    kv = lax.dot_general(ke, v, TN_DIMS, preferred_element_type=F32)
    dec = jnp.exp(jnp.broadcast_to(b_edge, (GLA_QK, GLA_QK)).T)
    state = jnp.concatenate([dec, dec], axis=1) * state + jnp.where(cs["st"], kv, 0.0)
    return o, state


def _gla_kernel(qf_ref, kf_ref, vf_ref, laf_ref, qb_ref, kb_ref, vb_ref, lab_ref, of_ref, ob_ref,
                sf_ref, sb_ref, *, nchunk):
    C = GLA_CHUNK

    @pl.when(pl.program_id(1) == 0)
    def _():
        sf_ref[...] = jnp.zeros_like(sf_ref)
        sb_ref[...] = jnp.zeros_like(sb_ref)

    cs = _gla_consts()

    def body(c, carry):
        s_f, s_b = carry
        rf = pl.ds(pl.multiple_of(c * C, C), C)
        o, s_f = _gla_chunk(qf_ref[0, rf, :].astype(F32), kf_ref[0, rf, :].astype(F32), vf_ref[0, rf, :],
                            laf_ref[0, rf, :], s_f, True, cs)
        of_ref[0, rf, :] = o
        rb = pl.ds(pl.multiple_of((nchunk - 1 - c) * C, C), C)
        o, s_b = _gla_chunk(qb_ref[0, rb, :].astype(F32), kb_ref[0, rb, :].astype(F32), vb_ref[0, rb, :],
                            lab_ref[0, rb, :], s_b, False, cs)
        ob_ref[0, rb, :] = o
        return s_f, s_b

    s_f, s_b = lax.fori_loop(0, nchunk, body, (sf_ref[...], sb_ref[...]))
    sf_ref[...] = s_f
    sb_ref[...] = s_b


def _gla(gq, gk, gv, la_f, la_b, tg):
    B, S, _ = gq.shape
    nb = S // tg
    fw = lambda w: pl.BlockSpec((1, tg, w), lambda b, i: (b, i, 0))
    bw = lambda w: pl.BlockSpec((1, tg, w), lambda b, i: (b, nb - 1 - i, 0))
    return pl.pallas_call(
        functools.partial(_gla_kernel, nchunk=tg // GLA_CHUNK),
        grid=(B, nb),
        in_specs=[fw(128), fw(128), fw(256), fw(128), bw(128), bw(128), bw(256), bw(128)],
        out_specs=[fw(256), bw(256)],
        out_shape=[jax.ShapeDtypeStruct((B, S, GLA_W), F32)] * 2,
        scratch_shapes=[pltpu.VMEM((GLA_QK, GLA_W), F32)] * 2,
        compiler_params=_params(("parallel", "arbitrary")), name="gla")(gq, gk, gv, la_f, gq, gk, gv, la_b)


def _fnet1_kernel(zr_ref, zi_ref, m1_ref, tc_ref, ts_ref, or_ref, oi_ref, *, ts1):
    n2 = zr_ref.shape[1]
    x = jnp.concatenate([zr_ref[0], zi_ref[0]], axis=0)
    z = _dot(m1_ref[...], x)
    zr, zi = z[:n2], z[n2:]
    tc, tsn = tc_ref[...], ts_ref[...]
    pr = zr * tc + zi * tsn
    pi = zi * tc - zr * tsn
    for j in range(ts1):
        sl = slice(j * FNET_W, (j + 1) * FNET_W)
        or_ref[0, j] = pr[:, sl].astype(BF16)
        oi_ref[0, j] = pi[:, sl].astype(BF16)


def _fnet2_kernel(zr_ref, zi_ref, m2_ref, o_ref):
    x = jnp.concatenate([zr_ref[0], zi_ref[0]], axis=0)
    o_ref[0] = _dot(m2_ref[...], x).astype(BF16)


def _fnet(zr, zi, tabs, ts1, tk2):
    B, S, CH = zr.shape
    n1, n2 = tabs["n1"], tabs["n2"]
    zr2 = zr.reshape(B, n2, n1 * CH)
    zi2 = zi.reshape(B, n2, n1 * CH)
    cols = ts1 * CH
    in_b = pl.BlockSpec((1, n2, cols), lambda b, i: (b, 0, i))
    tw_b = pl.BlockSpec((n2, cols), lambda b, i: (0, i))
    out_b = pl.BlockSpec((1, ts1, n2, CH), lambda b, i: (b, i, 0, 0))
    pr, pi = pl.pallas_call(
        functools.partial(_fnet1_kernel, ts1=ts1),
        grid=(B, n1 // ts1),
        in_specs=[in_b, in_b, _full(tabs["m1"].shape), tw_b, tw_b],
        out_specs=[out_b, out_b],
        out_shape=[jax.ShapeDtypeStruct((B, n1, n2, CH), BF16)] * 2,
        compiler_params=_params(("parallel", "parallel")), name="fnet1")(zr2, zi2, tabs["m1"], tabs["tw_c"],
                                                                         tabs["tw_s"])
    pr = pr.reshape(B, n1, n2 * CH)
    pi = pi.reshape(B, n1, n2 * CH)
    cols2 = tk2 * CH
    blk = pl.BlockSpec((1, n1, cols2), lambda b, i: (b, 0, i))
    out = pl.pallas_call(
        _fnet2_kernel, grid=(B, n2 // tk2),
        in_specs=[blk, blk, _full(tabs["m2"].shape)], out_specs=blk,
        out_shape=jax.ShapeDtypeStruct((B, n1, n2 * CH), BF16),
        compiler_params=_params(("parallel", "parallel")), name="fnet2")(pr, pi, tabs["m2"])
    return out.reshape(B, S, CH)


def _attn_kernel(qt_ref, k_ref, vt_ref, o_ref, s0_ref, s1_ref, s2_ref, *, tk, nk):
    qt = qt_ref[0, 0]
    tq = qt.shape[1]

    nc = tk // ATTN_CHUNK
    neg = jnp.full((1, tq), -1e30, F32)

    def step(j_new, s_new, j_cur, s_cur, mt_cur, m, acc):
        mt_new = neg
        if s_cur is not None:
            m_new = jnp.maximum(m, mt_cur)
            acc = jnp.exp2(m - m_new) * acc
            m = m_new
        for c in range(nc):
            rc = slice(c * ATTN_CHUNK, (c + 1) * ATTN_CHUNK)
            if s_new is not None:
                ks = pl.ds(pl.multiple_of(j_new * tk + c * ATTN_CHUNK, ATTN_CHUNK), ATTN_CHUNK)
                s = _dot(k_ref[0, 0, ks, :], qt)
                s_new[rc, :] = s
                mt_new = jnp.maximum(mt_new, jnp.max(s, axis=0, keepdims=True))
            if s_cur is not None:
                ks = pl.ds(pl.multiple_of(j_cur * tk + c * ATTN_CHUNK, ATTN_CHUNK), ATTN_CHUNK)
                p = jnp.exp2((s_cur[rc, :] - m).astype(BF16))
                acc = acc + _dot(vt_ref[0, 0, :, ks], p)
        return mt_new, m, acc

    bufs = (s0_ref, s1_ref, s2_ref)

    def body(i, carry):
        m, acc, mta, mtb = carry
        t = 3 * i
        mtc, m, acc = step(t + 2, bufs[2], t, bufs[0], mta, m, acc)
        mta, m, acc = step(t + 3, bufs[0], t + 1, bufs[1], mtb, m, acc)
        mtb, m, acc = step(t + 4, bufs[1], t + 2, bufs[2], mtc, m, acc)
        return m, acc, mta, mtb

    acc0 = jnp.zeros((MLA_V_ROWS, tq), F32)
    mta, _, _ = step(0, bufs[0], None, None, None, None, None)
    mtb, _, _ = step(1, bufs[1], None, None, None, None, None)
    nloop = (nk - 2) // 3
    m, acc, mta, mtb = lax.fori_loop(0, nloop, body, (neg, acc0, mta, mtb))
    mts = {3 * nloop: mta, 3 * nloop + 1: mtb}
    for t in range(3 * nloop, nk):
        new = t + 2 if t + 2 < nk else None
        mt_new, m, acc = step(new, None if new is None else bufs[new % 3], t, bufs[t % 3], mts[t], m, acc)
        mts[t + 2] = mt_new
    o_ref[0] = (acc[0:MLA_V] / acc[MLA_V:MLA_V + 1]).astype(BF16)


def _attn(qt, kk, vt, tq, tk):
    B, H, _, S = qt.shape
    assert S // tk >= 2 and tk % ATTN_CHUNK == 0
    return pl.pallas_call(
        functools.partial(_attn_kernel, tk=tk, nk=S // tk),
        grid=(B, H, S // tq),
        in_specs=[pl.BlockSpec((1, 1, MLA_QK_PAD, tq), lambda b, h, i: (b, h, 0, i)),
                  pl.BlockSpec((1, 1, S, MLA_QK_PAD), lambda b, h, i: (b, h, 0, 0)),
                  pl.BlockSpec((1, 1, MLA_V_ROWS, S), lambda b, h, i: (b, h, 0, 0))],
        out_specs=pl.BlockSpec((1, MLA_V, tq), lambda b, h, i: (b, h, i)),
        out_shape=jax.ShapeDtypeStruct((B, H * MLA_V, S), BF16),
        scratch_shapes=[pltpu.VMEM((tk, tq), F32)] * 3,
        compiler_params=_params(("parallel", "parallel", "arbitrary")), name="attn")(qt, kk, vt)


def _outproj_kernel(x_ref, of_ref, ob_ref, gg_ref, gng_ref, hm_ref, yf_ref, ot_ref, wo_ref, n2g_ref,
                    wrh_ref, wrl_ref, br_ref, ut_ref, lt_ref,
                    x1_ref, h_ref, info_ref, off_ref, cnt_ref):
    o = of_ref[0] + ob_ref[0]
    msq = _dot((o * o).astype(BF16), hm_ref[...])
    gate = gg_ref[0].astype(F32)
    og = o * lax.rsqrt(msq + EPS) * gng_ref[...] * (gate * jax.nn.sigmoid(gate))
    acc = _dot(og.astype(BF16), wo_ref[0:GLA_W, :])
    acc += _dot(yf_ref[0], wo_ref[GLA_W:GLA_W + FNET_W, :])
    acc += lax.dot_general(ot_ref[0], wo_ref[GLA_W + FNET_W:, :], TN_DIMS, preferred_element_type=F32)
    x1 = x_ref[0] + acc
    x1_ref[0] = x1
    h = _rms(x1, n2g_ref[...])
    h_hi = h.astype(BF16)
    h_ref[0] = h_hi
    h_lo = (h - h_hi.astype(F32)).astype(BF16)
    nt = lambda w, a: lax.dot_general(w, a, NT_DIMS, preferred_element_type=F32)
    lg = nt(wrh_ref[...], h_hi) + nt(wrh_ref[...], h_lo) + nt(wrl_ref[...], h_hi) + br_ref[...]
    tb = lg.shape[1]
    neg = jnp.float32(-jnp.inf)
    row8 = lax.broadcasted_iota(jnp.int32, (8, tb), 0)
    g8 = lg[128:136]
    gm = jnp.where(row8 < N_GROUPS, g8, neg)
    gmax = jnp.max(gm, axis=0, keepdims=True)
    gidx = jnp.min(jnp.where(gm == gmax, row8, 99), axis=0, keepdims=True)
    gsum = jnp.sum(jnp.where(row8 < N_GROUPS, jnp.exp(g8 - gmax), 0.0), axis=0, keepdims=True)
    grp_w = 1.0 / gsum
    row = lax.broadcasted_iota(jnp.int32, (N_EXPERTS, tb), 0)
    em = jnp.where((row // EXPERTS_PER_GROUP) == gidx, lg[0:N_EXPERTS], neg)
    e1 = jnp.max(em, axis=0, keepdims=True)
    i1 = jnp.min(jnp.where(em == e1, row, 99), axis=0, keepdims=True)
    em2 = jnp.where(row == i1, neg, em)
    e2 = jnp.max(em2, axis=0, keepdims=True)
    i2 = jnp.min(jnp.where(em2 == e2, row, 99), axis=0, keepdims=True)
    r = jnp.exp(e2 - e1)
    c1 = grp_w / (1.0 + r)
    c2 = grp_w * r / (1.0 + r)
    sel1 = row == i1
    sel2 = row == i2
    oh = jnp.where(sel1 | sel2, 1.0, 0.0)
    rank = _dot(oh.astype(BF16), ut_ref[...])
    cnt = jnp.sum(oh, axis=1, keepdims=True)
    padded = jnp.ceil(cnt * (1.0 / SEG_ALIGN)) * SEG_ALIGN
    off = _dot_hi(lt_ref[...], jnp.broadcast_to(padded, (N_EXPERTS, 128)))
    pos = off[:, 0:1] + rank
    pos1 = jnp.sum(jnp.where(sel1, pos, 0.0), axis=0, keepdims=True)
    pos2 = jnp.sum(jnp.where(sel2, pos, 0.0), axis=0, keepdims=True)
    info_ref[0] = jnp.concatenate([pos1, pos2, c1, c2, jnp.zeros((4, tb), F32)], axis=0)
    off_ref[0] = off.astype(jnp.int32)
    cnt_ref[0] = jnp.broadcast_to(cnt, (N_EXPERTS, 128)).astype(jnp.int32)


def _outproj(x, o_f, o_b, gg, yf, ot, lw, tabs, tb):
    B, S, D = x.shape
    nsb = S // tb
    tok = lambda w: pl.BlockSpec((1, tb, w), lambda b, i: (b, i, 0))
    weights_a = [lw["gng"], tabs["hm"]]
    weights_b = [lw["wo"], lw["n2g"], lw["wrh"], lw["wrl"], lw["br"], tabs["ut"], tabs["lt"]]
    in_specs = ([tok(D), tok(GLA_W), tok(GLA_W), tok(GLA_W)] + [_full(w.shape) for w in weights_a]
                + [tok(FNET_W), pl.BlockSpec((1, MLA_W, tb), lambda b, i: (b, 0, i))]
                + [_full(w.shape) for w in weights_b])
    blk = lambda r, c, dt: (jax.ShapeDtypeStruct((B * nsb, r, c), dt),
                            pl.BlockSpec((1, r, c), lambda b, i: (b * nsb + i, 0, 0)))
    outs = [(jax.ShapeDtypeStruct((B, S, D), F32), tok(D)), (jax.ShapeDtypeStruct((B, S, D), BF16), tok(D)),
            blk(8, tb, F32), blk(N_EXPERTS, 128, jnp.int32), blk(N_EXPERTS, 128, jnp.int32)]
    return pl.pallas_call(
        _outproj_kernel, grid=(B, nsb), in_specs=in_specs,
        out_specs=[o[1] for o in outs], out_shape=[o[0] for o in outs],
        compiler_params=_params(("parallel", "parallel")), name="outproj")(
            x, o_f, o_b, gg, *weights_a, yf, ot, *weights_b)


def _moe_kernel(off_ref, cnt_ref, h_ref, info_ref, wg_ref, wu_ref, wd_ref, y_ref,
                pm_ref, xs_ref, ow_ref, cs_ref):
    blk = pl.program_id(0)
    e = pl.program_id(1)
    ns, tb = pm_ref.shape

    @pl.when(e == 0)
    def _():
        info = info_ref[0]
        pos1 = info[0:1].astype(jnp.int32)
        pos2 = info[1:2].astype(jnp.int32)
        slot = lax.broadcasted_iota(jnp.int32, (ns, tb), 0)
        p1 = slot == pos1
        p2 = slot == pos2
        pm = jnp.where(p1 | p2, 1.0, 0.0).astype(BF16)
        pm_ref[...] = pm
        cs_ref[...] = jnp.sum(jnp.where(p1, info[2:3], 0.0) + jnp.where(p2, info[3:4], 0.0), axis=1,
                              keepdims=True)
        xs_ref[...] = _dot(pm, h_ref[...]).astype(BF16)
        ow_ref[...] = jnp.zeros_like(ow_ref)

    off_e = off_ref[blk * N_EXPERTS + e]
    ntile = (cnt_ref[blk * N_EXPERTS + e] + ROW_TILE - 1) // ROW_TILE

    def tile(i, carry):
        rows = pl.ds(pl.multiple_of(off_e + i * ROW_TILE, SEG_ALIGN), ROW_TILE)
        xt = xs_ref[rows, :]
        g = _dot(xt, wg_ref[0])
        u = _dot(xt, wu_ref[0])
        hh = (g * jax.nn.sigmoid(g) * u).astype(BF16)
        o = _dot(hh, wd_ref[0])
        ow_ref[rows, :] = (o * cs_ref[rows, :]).astype(BF16)
        return carry

    lax.fori_loop(0, ntile, tile, 0)

    @pl.when(e == N_EXPERTS - 1)
    def _():
        y_ref[...] = lax.dot_general(pm_ref[...], ow_ref[...], TN_DIMS, preferred_element_type=F32)


def _moe(h, info, off, cnt, lw, tb):
    T, D = h.shape
    nblk = T // tb
    ns = 2 * tb + N_EXPERTS * SEG_ALIGN + ROW_TILE
    grid_spec = pltpu.PrefetchScalarGridSpec(
        num_scalar_prefetch=2, grid=(nblk, N_EXPERTS),
        in_specs=[pl.BlockSpec((tb, D), lambda b, e, *_: (b, 0)),
                  pl.BlockSpec((1, 8, tb), lambda b, e, *_: (b, 0, 0)),
                  pl.BlockSpec((1, D, D_EXPERT), lambda b, e, *_: (e, 0, 0)),
                  pl.BlockSpec((1, D, D_EXPERT), lambda b, e, *_: (e, 0, 0)),
                  pl.BlockSpec((1, D_EXPERT, D), lambda b, e, *_: (e, 0, 0))],
        out_specs=pl.BlockSpec((tb, D), lambda b, e, *_: (b, 0)),
        scratch_shapes=[pltpu.VMEM((ns, tb), BF16), pltpu.VMEM((ns, D), BF16), pltpu.VMEM((ns, D), BF16),
                        pltpu.VMEM((ns, 1), F32)])
    return pl.pallas_call(
        _moe_kernel, grid_spec=grid_spec, out_shape=jax.ShapeDtypeStruct((T, D), F32),
        compiler_params=_params(("parallel", "arbitrary")), name="moe")(
            off, cnt, h, info, lw["wg"], lw["wu"], lw["wd"])


def _final_kernel(x_ref, y_ref, g_ref, o_ref):
    o_ref[...] = _rms(x_ref[...] + y_ref[...], g_ref[...])


def _final(x, y, g, tm):
    T, D = x.shape
    blk = pl.BlockSpec((tm, D), lambda i: (i, 0))
    return pl.pallas_call(
        _final_kernel, grid=(T // tm,), in_specs=[blk, blk, _full((1, D))], out_specs=blk,
        out_shape=jax.ShapeDtypeStruct((T, D), F32),
        compiler_params=_params(("parallel",)), name="final")(x, y, g)


def _block_diag(blocks):
    n = len(blocks)
    r, c = blocks[0].shape
    out = jnp.zeros((n * r, n * c), blocks[0].dtype)
    for i, b in enumerate(blocks):
        out = out.at[i * r:(i + 1) * r, i * c:(i + 1) * c].set(b)
    return out


def _tables(positions, S, tb):
    B = positions.shape[0]
    n1 = 1 << ((S.bit_length() - 1 + 1) // 2)
    n2 = S // n1
    assert n1 * n2 == S
    half = MLA_ROPE // 2
    inv = 1.0 / (ROPE_THETA ** (jnp.arange(0, MLA_ROPE, 2, dtype=F32) / MLA_ROPE))
    ang = positions.astype(F32)[..., None] * inv
    cos, sin = jnp.cos(ang), jnp.sin(ang)
    z = lambda n: jnp.zeros((B, S, n), F32)
    pad = MLA_QK_PAD - MLA_NOPE - MLA_ROPE
    cos_q = jnp.concatenate([jnp.ones((B, S, MLA_NOPE), F32), cos, cos, z(pad)], -1)
    sin_q = jnp.concatenate([z(MLA_NOPE), -sin, sin, z(pad)], -1)
    cos_k = jnp.concatenate([z(MLA_NOPE), cos, cos, z(pad)], -1)
    dft = lambda n: 2.0 * np.pi * np.outer(np.arange(n), np.arange(n)) / n
    a64 = dft(FNET_DG)
    c64 = jnp.asarray(np.cos(a64) / math.sqrt(FNET_DG), F32)
    s64 = jnp.asarray(np.sin(a64) / math.sqrt(FNET_DG), F32)
    a2, a1 = dft(n2), dft(n1)
    c2, s2 = np.cos(a2) / math.sqrt(n2), np.sin(a2) / math.sqrt(n2)
    c1, s1 = np.cos(a1) / math.sqrt(n1), np.sin(a1) / math.sqrt(n1)
    m1 = np.block([[c2, s2], [-s2, c2]])
    m2 = np.concatenate([c1, s1], axis=1)
    kk = jnp.arange(n2, dtype=jnp.int32)[:, None] * jnp.arange(n1, dtype=jnp.int32)[None, :]
    th = (kk % S).astype(F32) * (2.0 * math.pi / S)
    hm = np.kron(np.eye(GLA_HEADS), np.full((GLA_DV, GLA_DV), 1.0 / GLA_DV))
    tt = np.arange(tb)
    return dict(
        n1=n1, n2=n2,
        cos_t=cos_q.transpose(0, 2, 1), sin_t=sin_q.transpose(0, 2, 1), cos_k=cos_k, sin_k=sin_q,
        cbd=_block_diag([c64] * FNET_GROUPS).astype(BF16), sbd=_block_diag([s64] * FNET_GROUPS).astype(BF16),
        m1=jnp.asarray(m1, BF16), m2=jnp.asarray(m2, BF16),
        tw_c=jnp.repeat(jnp.cos(th), FNET_W, axis=1), tw_s=jnp.repeat(jnp.sin(th), FNET_W, axis=1),
        hm=jnp.asarray(hm, BF16),
        ut=jnp.asarray(tt[:, None] < tt[None, :], BF16),
        lt=jnp.asarray(np.arange(N_EXPERTS)[None, :] < np.arange(N_EXPERTS)[:, None], F32))


def _layer_weights(l, p):
    w_in = p["w_in"][l]
    D = w_in.shape[0]
    o = np.cumsum([0, GLA_QK, GLA_QK, GLA_W, GLA_W, GLA_RANK, GLA_RANK, FNET_W, MLA_Q_RANK, MLA_KV_RANK, MLA_ROPE])
    col = lambda i: w_in[:, int(o[i]):int(o[i + 1])]
    zc = lambda n: jnp.zeros((D, n), F32)
    half = MLA_ROPE // 2
    kr = col(9)
    pad = MLA_QK_PAD - MLA_NOPE - MLA_ROPE
    kr1 = jnp.concatenate([zc(MLA_NOPE), kr[:, :half], kr[:, half:], zc(pad)], 1)
    kr2 = jnp.concatenate([zc(MLA_NOPE), kr[:, half:], kr[:, :half], zc(pad)], 1)
    w1 = jnp.concatenate([col(0), col(1), col(2), col(3), col(4), col(5), zc(128 - 2 * GLA_RANK), col(6), col(7),
                          col(8), kr1, kr2], 1).astype(BF16)
    wab = jnp.zeros((128, 2 * GLA_QK), F32)
    wab = wab.at[0:GLA_RANK, 0:GLA_QK].set(p["gla_wa_f"][l])
    wab = wab.at[GLA_RANK:2 * GLA_RANK, GLA_QK:].set(p["gla_wa_b"][l])
    bab = jnp.concatenate([p["gla_ba_f"][l], p["gla_ba_b"][l]])[None, :]
    wuq = p["mla_w_uq"][l].reshape(MLA_Q_RANK, MLA_HEADS, MLA_NOPE + MLA_ROPE)
    zq = lambda n: jnp.zeros((MLA_Q_RANK, MLA_HEADS, n), F32)
    nope, x1, x2 = wuq[..., :MLA_NOPE], wuq[..., MLA_NOPE:MLA_NOPE + half], wuq[..., MLA_NOPE + half:]
    wq1 = jnp.concatenate([nope, x1, x2, zq(pad)], -1).reshape(MLA_Q_RANK, -1)
    wq2 = jnp.concatenate([zq(MLA_NOPE), x2, x1, zq(pad)], -1).reshape(MLA_Q_RANK, -1)
    wukv = p["mla_w_ukv"][l].reshape(MLA_KV_RANK, MLA_HEADS, MLA_NOPE + MLA_V)
    wk = jnp.concatenate([wukv[..., :MLA_NOPE], jnp.zeros((MLA_KV_RANK, MLA_HEADS, MLA_QK_PAD - MLA_NOPE), F32)],
                         -1).reshape(MLA_KV_RANK, -1)
    wvt = wukv[..., MLA_NOPE:].reshape(MLA_KV_RANK, -1).T
    wr = jnp.zeros((256, D), F32)
    wr = wr.at[0:N_EXPERTS].set(p["moe_w_re"][l].T).at[128:128 + N_GROUPS].set(p["moe_w_rg"][l].T)
    br = jnp.zeros((256, 1), F32)
    br = br.at[0:N_EXPERTS, 0].set(p["moe_b_re"][l]).at[128:128 + N_GROUPS, 0].set(p["moe_b_rg"][l])
    wrh = wr.astype(BF16)
    return dict(
        g1=p["norm1_g"][l][None, :], w1=w1, wab=wab.astype(BF16), bab=bab,
        wmix=_block_diag([p["fnet_w"][l, g] for g in range(FNET_GROUPS)]).astype(BF16),
        qg=p["mla_q_norm_g"][l][None, :], wq1=wq1.T.astype(BF16), wq2=wq2.T.astype(BF16),
        kvg=p["mla_kv_norm_g"][l][None, :], wk=wk.astype(BF16), wvt=wvt.astype(BF16),
        gng=jnp.tile(p["gla_norm_g"][l], GLA_HEADS)[None, :], wo=p["w_out"][l].astype(BF16),
        n2g=p["norm2_g"][l][None, :], wrh=wrh, wrl=(wr - wrh.astype(F32)).astype(BF16), br=br,
        wg=p["moe_w_gate"][l].astype(BF16), wu=p["moe_w_up"][l].astype(BF16), wd=p["moe_w_down"][l].astype(BF16))


def _tiles(S):
    return dict(tm=min(512, S), tg=min(512, S), ts1=8, tk2=8, tq=min(512, S), tk=min(1024, S // 2), tb=min(1024, S))


def kernel(x, positions, norm1_g, w_in, gla_wa_f, gla_ba_f, gla_wa_b, gla_ba_b, gla_norm_g, fnet_w, mla_q_norm_g,
           mla_w_uq, mla_kv_norm_g, mla_w_ukv, w_out, norm2_g, moe_w_rg, moe_b_rg, moe_w_re, moe_b_re, moe_w_gate,
           moe_w_up, moe_w_down, final_norm_g):
    p = dict(norm1_g=norm1_g, w_in=w_in, gla_wa_f=gla_wa_f, gla_ba_f=gla_ba_f, gla_wa_b=gla_wa_b,
             gla_ba_b=gla_ba_b, gla_norm_g=gla_norm_g, fnet_w=fnet_w, mla_q_norm_g=mla_q_norm_g,
             mla_w_uq=mla_w_uq, mla_kv_norm_g=mla_kv_norm_g, mla_w_ukv=mla_w_ukv, w_out=w_out, norm2_g=norm2_g,
             moe_w_rg=moe_w_rg, moe_b_rg=moe_b_rg, moe_w_re=moe_w_re, moe_b_re=moe_b_re, moe_w_gate=moe_w_gate,
             moe_w_up=moe_w_up, moe_w_down=moe_w_down)
    B, S, D = x.shape
    t = _tiles(S)
    tabs = _tables(positions, S, t["tb"])
    y = None
    for l in range(w_in.shape[0]):
        lw = _layer_weights(l, p)
        x, (gq, gk, gv, gg, la_f, la_b, zr, zi, qt, kk, vt) = _proj(x, y, lw, tabs, t["tm"])
        o_f, o_b = _gla(gq, gk, gv, la_f, la_b, t["tg"])
        yf = _fnet(zr, zi, tabs, min(t["ts1"], tabs["n1"]), min(t["tk2"], tabs["n2"]))
        ot = _attn(qt, kk, vt, t["tq"], t["tk"])
        x, h, info, off, cnt = _outproj(x, o_f, o_b, gg, yf, ot, lw, tabs, t["tb"])
        y = _moe(h.reshape(B * S, D), info, off[:, :, 0].reshape(-1), cnt[:, :, 0].reshape(-1), lw, t["tb"])
        y = y.reshape(B, S, D)
    return _final(x.reshape(B * S, D), y.reshape(B * S, D), final_norm_g[None, :], t["tm"]).reshape(B, S, D)
```

```python
import functools
import math

import numpy as np
import jax
import jax.numpy as jnp
from jax import lax
from jax.experimental import pallas as pl
from jax.experimental.pallas import tpu as pltpu

F32 = jnp.float32
BF16 = jnp.bfloat16
EPS = 1e-6

GLA_HEADS, GLA_DK, GLA_DV, GLA_QK, GLA_W, GLA_RANK = 4, 32, 64, 128, 256, 16
GLA_TAU, GLA_CHUNK = 16.0, 64
FNET_GROUPS, FNET_DG, FNET_W = 4, 64, 256
MLA_HEADS, MLA_NOPE, MLA_ROPE, MLA_V = 8, 64, 32, 64
MLA_Q_RANK, MLA_KV_RANK, MLA_W = 256, 128, 512
MLA_QK_PAD = 128
MLA_V_ROWS = 80
ATTN_CHUNK = 512
ROPE_THETA = 10000.0
N_GROUPS, EXPERTS_PER_GROUP, N_EXPERTS, D_EXPERT = 4, 8, 32, 256

SEG_ALIGN = 16
ROW_TILE = 128
VMEM_LIMIT = 56 * 1024 * 1024

NT_DIMS = (((1,), (1,)), ((), ()))
TN_DIMS = (((0,), (0,)), ((), ()))


def _dot(a, b):
    return jnp.dot(a, b, preferred_element_type=F32)


def _dot_hi(a, b):
    return jnp.dot(a, b, preferred_element_type=F32, precision=lax.Precision.HIGHEST)


def _rms(x, g):
    return x * lax.rsqrt(jnp.mean(x * x, axis=-1, keepdims=True) + EPS) * g


def _params(sem):
    return pltpu.CompilerParams(dimension_semantics=sem, vmem_limit_bytes=VMEM_LIMIT)


def _full(shape):
    return pl.BlockSpec(shape, lambda *_: (0,) * len(shape))


def _proj_kernel(has_res, *refs):
    if has_res:
        x_ref, y_ref = refs[:2]
        refs = refs[2:]
    else:
        x_ref = refs[0]
        refs = refs[1:]
    (g1_ref, w1_ref, wab_ref, bab_ref, cbd_ref, sbd_ref, wmix_ref, qg_ref, wq1_ref, wq2_ref,
     cost_ref, sint_ref, kvg_ref, wk_ref, cosk_ref, sink_ref, wvt_ref) = refs[:17]
    outs = refs[17:]
    if has_res:
        xs_ref = outs[0]
        outs = outs[1:]
    (gq_ref, gk_ref, gv_ref, gg_ref, laf_ref, lab_ref, zr_ref, zi_ref, qt_ref, kk_ref, vt_ref) = outs

    x = x_ref[0]
    if has_res:
        x = x + y_ref[0]
        xs_ref[0] = x
    xn = _rms(x, g1_ref[...]).astype(BF16)
    p = _dot(xn, w1_ref[...])
    gq_ref[0] = (p[:, 0:128] * (GLA_DK ** -0.5)).astype(BF16)
    gk_ref[0] = p[:, 128:256].astype(BF16)
    gv_ref[0] = p[:, 256:512].astype(BF16)
    gg_ref[0] = p[:, 512:768].astype(BF16)
    pre = _dot(p[:, 768:896].astype(BF16), wab_ref[...]) + bab_ref[...]
    la = (jnp.minimum(pre, 0.0) - jnp.log(1.0 + jnp.exp(-jnp.abs(pre)))) * (1.0 / GLA_TAU)
    laf_ref[0] = la[:, 0:128]
    lab_ref[0] = la[:, 128:256]
    u = p[:, 896:1152].astype(BF16)
    zr_ref[0] = _dot(_dot(u, cbd_ref[...]).astype(BF16), wmix_ref[...]).astype(BF16)
    zi_ref[0] = (-_dot(_dot(u, sbd_ref[...]).astype(BF16), wmix_ref[...])).astype(BF16)
    cqn = _rms(p[:, 1152:1408], qg_ref[...]).astype(BF16)
    qa = lax.dot_general(wq1_ref[...], cqn, NT_DIMS, preferred_element_type=F32)
    qb = lax.dot_general(wq2_ref[...], cqn, NT_DIMS, preferred_element_type=F32)
    cos_t = cost_ref[0]
    sin_t = sint_ref[0]
    scale = (MLA_NOPE + MLA_ROPE) ** -0.5 * math.log2(math.e)
    for h in range(MLA_HEADS):
        sl = slice(h * MLA_QK_PAD, (h + 1) * MLA_QK_PAD)
        qt_ref[0, h] = ((qa[sl] * cos_t + qb[sl] * sin_t) * scale).astype(BF16)
    kvn = _rms(p[:, 1408:1536], kvg_ref[...]).astype(BF16)
    kf = _dot(kvn, wk_ref[...])
    krope = p[:, 1536:1664] * cosk_ref[0] + p[:, 1664:1792] * sink_ref[0]
    for h in range(MLA_HEADS):
        sl = slice(h * MLA_QK_PAD, (h + 1) * MLA_QK_PAD)
        kk_ref[0, h] = (kf[:, sl] + krope).astype(BF16)
    vt = lax.dot_general(wvt_ref[...], kvn, NT_DIMS, preferred_element_type=F32)
    tm = vt.shape[1]
    ones = jnp.ones((MLA_V_ROWS - MLA_V, tm), BF16)
    for h in range(MLA_HEADS):
        vt_ref[0, h, 0:MLA_V, :] = vt[h * MLA_V:(h + 1) * MLA_V].astype(BF16)
        vt_ref[0, h, MLA_V:MLA_V_ROWS, :] = ones


def _proj(x, y, lw, tabs, tm):
    B, S, D = x.shape
    has_res = y is not None
    H = MLA_HEADS
    tok = lambda w: pl.BlockSpec((1, tm, w), lambda b, i: (b, i, 0))
    in_specs = [tok(D)] + ([tok(D)] if has_res else [])
    weights = [lw["g1"], lw["w1"], lw["wab"], lw["bab"], tabs["cbd"], tabs["sbd"], lw["wmix"], lw["qg"],
               lw["wq1"], lw["wq2"]]
    in_specs += [_full(w.shape) for w in weights]
    in_specs += [pl.BlockSpec((1, MLA_QK_PAD, tm), lambda b, i: (b, 0, i))] * 2
    in_specs += [_full(lw["kvg"].shape), _full(lw["wk"].shape)]
    in_specs += [tok(MLA_QK_PAD)] * 2
    in_specs += [_full(lw["wvt"].shape)]
    args = [x] + ([y] if has_res else []) + weights + [tabs["cos_t"], tabs["sin_t"], lw["kvg"], lw["wk"],
                                                       tabs["cos_k"], tabs["sin_k"], lw["wvt"]]
    out_shape, out_specs = [], []
    if has_res:
        out_shape.append(jax.ShapeDtypeStruct((B, S, D), F32))
        out_specs.append(tok(D))
    for w, dt in ((128, BF16), (128, BF16), (256, BF16), (256, BF16), (128, F32), (128, F32), (256, BF16),
                  (256, BF16)):
        out_shape.append(jax.ShapeDtypeStruct((B, S, w), dt))
        out_specs.append(tok(w))
    out_shape.append(jax.ShapeDtypeStruct((B, H, MLA_QK_PAD, S), BF16))
    out_specs.append(pl.BlockSpec((1, H, MLA_QK_PAD, tm), lambda b, i: (b, 0, 0, i)))
    out_shape.append(jax.ShapeDtypeStruct((B, H, S, MLA_QK_PAD), BF16))
    out_specs.append(pl.BlockSpec((1, H, tm, MLA_QK_PAD), lambda b, i: (b, 0, i, 0)))
    out_shape.append(jax.ShapeDtypeStruct((B, H, MLA_V_ROWS, S), BF16))
    out_specs.append(pl.BlockSpec((1, H, MLA_V_ROWS, tm), lambda b, i: (b, 0, 0, i)))
    res = pl.pallas_call(
        functools.partial(_proj_kernel, has_res),
        grid=(B, S // tm), in_specs=in_specs, out_specs=out_specs, out_shape=out_shape,
        compiler_params=_params(("parallel", "parallel")), name="proj")(*args)
    res = list(res)
    xs = res.pop(0) if has_res else x
    return xs, res


def _gla_consts():
    C, W, QK = GLA_CHUNK, GLA_W, GLA_QK
    i64 = lax.broadcasted_iota(jnp.int32, (C, C), 0)
    j64 = lax.broadcasted_iota(jnp.int32, (C, C), 1)
    r_att = lax.broadcasted_iota(jnp.int32, (C, W), 0)
    c_att = lax.broadcasted_iota(jnp.int32, (C, W), 1) % C
    r_ks = lax.broadcasted_iota(jnp.int32, (W, QK), 0) // C
    c_ks = lax.broadcasted_iota(jnp.int32, (W, QK), 1) // GLA_DK
    r_bd = lax.broadcasted_iota(jnp.int32, (W, W), 0) // C
    c_bd = lax.broadcasted_iota(jnp.int32, (W, W), 1) // GLA_DV
    r_s = lax.broadcasted_iota(jnp.int32, (QK, W), 0) // GLA_DK
    c_s = lax.broadcasted_iota(jnp.int32, (QK, W), 1) // GLA_DV
    return dict(
        tri_f=(j64 <= i64).astype(F32), tri_b=(j64 >= i64).astype(F32),
        att_f=c_att <= r_att, att_b=c_att > r_att,
        ks=r_ks == c_ks, bd=r_bd == c_bd, st=r_s == c_s)


def _gla_chunk(q, k, v, la, state, fwd, cs):
    C = GLA_CHUNK
    b = _dot_hi(cs["tri_f"] if fwd else cs["tri_b"], la)
    b_edge = b[C - 1:C] if fwd else b[0:1]
    qd = (q * jnp.exp(b)).astype(BF16)
    ki = (k * jnp.exp(-b)).astype(BF16)
    ke = (k * jnp.exp(b_edge - b)).astype(BF16)
    kstack = jnp.where(cs["ks"], jnp.concatenate([ki] * GLA_HEADS, axis=0), jnp.zeros((), BF16))
    att = lax.dot_general(qd, kstack, NT_DIMS, preferred_element_type=F32)
    att = jnp.where(cs["att_f"] if fwd else cs["att_b"], att, 0.0).astype(BF16)
    vbd = jnp.where(cs["bd"], jnp.concatenate([v] * GLA_HEADS, axis=0), jnp.zeros((), BF16))
    o = _dot(att, vbd) + _dot(qd, state.astype(BF16))
    kv = lax.dot_general(ke, v, TN_DIMS, preferred_element_type=F32)
    dec = jnp.exp(jnp.broadcast_to(b_edge, (GLA_QK, GLA_QK)).T)
    state = jnp.concatenate([dec, dec], axis=1) * state + jnp.where(cs["st"], kv, 0.0)
    return o, state


def _gla_kernel(qf_ref, kf_ref, vf_ref, laf_ref, qb_ref, kb_ref, vb_ref, lab_ref, of_ref, ob_ref,
                sf_ref, sb_ref, *, nchunk):
    C = GLA_CHUNK
    nbatch = sf_ref.shape[0]

    @pl.when(pl.program_id(0) == 0)
    def _():
        sf_ref[...] = jnp.zeros_like(sf_ref)
        sb_ref[...] = jnp.zeros_like(sb_ref)

    cs = _gla_consts()

    def body(c, carry):
        s_f, s_b = list(carry[0]), list(carry[1])
        rf = pl.ds(pl.multiple_of(c * C, C), C)
        rb = pl.ds(pl.multiple_of((nchunk - 1 - c) * C, C), C)
        for b in range(nbatch):
            o, s_f[b] = _gla_chunk(qf_ref[b, rf, :].astype(F32), kf_ref[b, rf, :].astype(F32), vf_ref[b, rf, :],
                                   laf_ref[b, rf, :], s_f[b], True, cs)
            of_ref[b, rf, :] = o
            o, s_b[b] = _gla_chunk(qb_ref[b, rb, :].astype(F32), kb_ref[b, rb, :].astype(F32), vb_ref[b, rb, :],
                                   lab_ref[b, rb, :], s_b[b], False, cs)
            ob_ref[b, rb, :] = o
        return tuple(s_f), tuple(s_b)

    init = (tuple(sf_ref[b] for b in range(nbatch)), tuple(sb_ref[b] for b in range(nbatch)))
    s_f, s_b = lax.fori_loop(0, nchunk, body, init)
    for b in range(nbatch):
        sf_ref[b] = s_f[b]
        sb_ref[b] = s_b[b]


def _gla(gq, gk, gv, la_f, la_b, tg):
    B, S, _ = gq.shape
    nb = S // tg
    fw = lambda w: pl.BlockSpec((B, tg, w), lambda i: (0, i, 0))
    bw = lambda w: pl.BlockSpec((B, tg, w), lambda i: (0, nb - 1 - i, 0))
    return pl.pallas_call(
        functools.partial(_gla_kernel, nchunk=tg // GLA_CHUNK),
        grid=(nb,),
        in_specs=[fw(128), fw(128), fw(256), fw(128), bw(128), bw(128), bw(256), bw(128)],
        out_specs=[fw(256), bw(256)],
        out_shape=[jax.ShapeDtypeStruct((B, S, GLA_W), F32)] * 2,
        scratch_shapes=[pltpu.VMEM((B, GLA_QK, GLA_W), F32)] * 2,
        compiler_params=_params(("arbitrary",)), name="gla")(gq, gk, gv, la_f, gq, gk, gv, la_b)


def _fnet1_kernel(zr_ref, zi_ref, m1_ref, tc_ref, ts_ref, or_ref, oi_ref, *, ts1):
    n2 = zr_ref.shape[1]
    x = jnp.concatenate([zr_ref[0], zi_ref[0]], axis=0)
    z = _dot(m1_ref[...], x)
    zr, zi = z[:n2], z[n2:]
    tc, tsn = tc_ref[...], ts_ref[...]
    pr = zr * tc + zi * tsn
    pi = zi * tc - zr * tsn
    for j in range(ts1):
        sl = slice(j * FNET_W, (j + 1) * FNET_W)
        or_ref[0, j] = pr[:, sl].astype(BF16)
        oi_ref[0, j] = pi[:, sl].astype(BF16)


def _fnet2_kernel(zr_ref, zi_ref, m2_ref, o_ref):
    x = jnp.concatenate([zr_ref[0], zi_ref[0]], axis=0)
    o_ref[0] = _dot(m2_ref[...], x).astype(BF16)


def _fnet(zr, zi, tabs, ts1, tk2):
    B, S, CH = zr.shape
    n1, n2 = tabs["n1"], tabs["n2"]
    zr2 = zr.reshape(B, n2, n1 * CH)
    zi2 = zi.reshape(B, n2, n1 * CH)
    cols = ts1 * CH
    in_b = pl.BlockSpec((1, n2, cols), lambda b, i: (b, 0, i))
    tw_b = pl.BlockSpec((n2, cols), lambda b, i: (0, i))
    out_b = pl.BlockSpec((1, ts1, n2, CH), lambda b, i: (b, i, 0, 0))
    pr, pi = pl.pallas_call(
        functools.partial(_fnet1_kernel, ts1=ts1),
        grid=(B, n1 // ts1),
        in_specs=[in_b, in_b, _full(tabs["m1"].shape), tw_b, tw_b],
        out_specs=[out_b, out_b],
        out_shape=[jax.ShapeDtypeStruct((B, n1, n2, CH), BF16)] * 2,
        compiler_params=_params(("parallel", "parallel")), name="fnet1")(zr2, zi2, tabs["m1"], tabs["tw_c"],
                                                                         tabs["tw_s"])
    pr = pr.reshape(B, n1, n2 * CH)
    pi = pi.reshape(B, n1, n2 * CH)
    cols2 = tk2 * CH
    blk = pl.BlockSpec((1, n1, cols2), lambda b, i: (b, 0, i))
    out = pl.pallas_call(
        _fnet2_kernel, grid=(B, n2 // tk2),
        in_specs=[blk, blk, _full(tabs["m2"].shape)], out_specs=blk,
        out_shape=jax.ShapeDtypeStruct((B, n1, n2 * CH), BF16),
        compiler_params=_params(("parallel", "parallel")), name="fnet2")(pr, pi, tabs["m2"])
    return out.reshape(B, S, CH)


def _attn_kernel(qt_ref, k_ref, vt_ref, o_ref, s0_ref, s1_ref, s2_ref, *, tk, nk):
    qt = qt_ref[0, 0]
    tq = qt.shape[1]

    nc = tk // ATTN_CHUNK
    neg = jnp.full((1, tq), -1e30, F32)

    def step(j_new, s_new, j_cur, s_cur, mt_cur, m, acc):
        mt_new = neg
        if s_cur is not None:
            m_new = jnp.maximum(m, mt_cur)
            acc = jnp.exp2(m - m_new) * acc
            m = m_new
        for c in range(nc):
            rc = slice(c * ATTN_CHUNK, (c + 1) * ATTN_CHUNK)
            if s_new is not None:
                ks = pl.ds(pl.multiple_of(j_new * tk + c * ATTN_CHUNK, ATTN_CHUNK), ATTN_CHUNK)
                s = _dot(k_ref[0, 0, ks, :], qt)
                s_new[rc, :] = s
                mt_new = jnp.maximum(mt_new, jnp.max(s, axis=0, keepdims=True))
            if s_cur is not None:
                ks = pl.ds(pl.multiple_of(j_cur * tk + c * ATTN_CHUNK, ATTN_CHUNK), ATTN_CHUNK)
                p = jnp.exp2((s_cur[rc, :] - m).astype(BF16))
                acc = acc + _dot(vt_ref[0, 0, :, ks], p)
        return mt_new, m, acc

    bufs = (s0_ref, s1_ref, s2_ref)

    def body(i, carry):
        m, acc, mta, mtb = carry
        t = 3 * i
        mtc, m, acc = step(t + 2, bufs[2], t, bufs[0], mta, m, acc)
        mta, m, acc = step(t + 3, bufs[0], t + 1, bufs[1], mtb, m, acc)
        mtb, m, acc = step(t + 4, bufs[1], t + 2, bufs[2], mtc, m, acc)
        return m, acc, mta, mtb

    acc0 = jnp.zeros((MLA_V_ROWS, tq), F32)
    mta, _, _ = step(0, bufs[0], None, None, None, None, None)
    mtb, _, _ = step(1, bufs[1], None, None, None, None, None)
    nloop = (nk - 2) // 3
    m, acc, mta, mtb = lax.fori_loop(0, nloop, body, (neg, acc0, mta, mtb))
    mts = {3 * nloop: mta, 3 * nloop + 1: mtb}
    for t in range(3 * nloop, nk):
        new = t + 2 if t + 2 < nk else None
        mt_new, m, acc = step(new, None if new is None else bufs[new % 3], t, bufs[t % 3], mts[t], m, acc)
        mts[t + 2] = mt_new
    o_ref[0] = (acc[0:MLA_V] / acc[MLA_V:MLA_V + 1]).astype(BF16)


def _attn(qt, kk, vt, tq, tk):
    B, H, _, S = qt.shape
    assert S // tk >= 2 and tk % ATTN_CHUNK == 0
    return pl.pallas_call(
        functools.partial(_attn_kernel, tk=tk, nk=S // tk),
        grid=(B, H, S // tq),
        in_specs=[pl.BlockSpec((1, 1, MLA_QK_PAD, tq), lambda b, h, i: (b, h, 0, i)),
                  pl.BlockSpec((1, 1, S, MLA_QK_PAD), lambda b, h, i: (b, h, 0, 0)),
                  pl.BlockSpec((1, 1, MLA_V_ROWS, S), lambda b, h, i: (b, h, 0, 0))],
        out_specs=pl.BlockSpec((1, MLA_V, tq), lambda b, h, i: (b, h, i)),
        out_shape=jax.ShapeDtypeStruct((B, H * MLA_V, S), BF16),
        scratch_shapes=[pltpu.VMEM((tk, tq), F32)] * 3,
        compiler_params=_params(("parallel", "parallel", "arbitrary")), name="attn")(qt, kk, vt)


def _outproj_kernel(x_ref, of_ref, ob_ref, gg_ref, gng_ref, hm_ref, yf_ref, ot_ref, wo_ref, n2g_ref,
                    wrh_ref, wrl_ref, br_ref, ut_ref, lt_ref,
                    x1_ref, h_ref, info_ref, off_ref, cnt_ref):
    o = of_ref[0] + ob_ref[0]
    msq = _dot((o * o).astype(BF16), hm_ref[...])
    gate = gg_ref[0].astype(F32)
    og = o * lax.rsqrt(msq + EPS) * gng_ref[...] * (gate * jax.nn.sigmoid(gate))
    acc = _dot(og.astype(BF16), wo_ref[0:GLA_W, :])
    acc += _dot(yf_ref[0], wo_ref[GLA_W:GLA_W + FNET_W, :])
    acc += lax.dot_general(ot_ref[0], wo_ref[GLA_W + FNET_W:, :], TN_DIMS, preferred_element_type=F32)
    x1 = x_ref[0] + acc
    x1_ref[0] = x1
    h = _rms(x1, n2g_ref[...])
    h_hi = h.astype(BF16)
    h_ref[0] = h_hi
    h_lo = (h - h_hi.astype(F32)).astype(BF16)
    nt = lambda w, a: lax.dot_general(w, a, NT_DIMS, preferred_element_type=F32)
    lg = nt(wrh_ref[...], h_hi) + nt(wrh_ref[...], h_lo) + nt(wrl_ref[...], h_hi) + br_ref[...]
    tb = lg.shape[1]
    neg = jnp.float32(-jnp.inf)
    row8 = lax.broadcasted_iota(jnp.int32, (8, tb), 0)
    g8 = lg[128:136]
    gm = jnp.where(row8 < N_GROUPS, g8, neg)
    gmax = jnp.max(gm, axis=0, keepdims=True)
    gidx = jnp.min(jnp.where(gm == gmax, row8, 99), axis=0, keepdims=True)
    gsum = jnp.sum(jnp.where(row8 < N_GROUPS, jnp.exp(g8 - gmax), 0.0), axis=0, keepdims=True)
    grp_w = 1.0 / gsum
    row = lax.broadcasted_iota(jnp.int32, (N_EXPERTS, tb), 0)
    em = jnp.where((row // EXPERTS_PER_GROUP) == gidx, lg[0:N_EXPERTS], neg)
    e1 = jnp.max(em, axis=0, keepdims=True)
    i1 = jnp.min(jnp.where(em == e1, row, 99), axis=0, keepdims=True)
    em2 = jnp.where(row == i1, neg, em)
    e2 = jnp.max(em2, axis=0, keepdims=True)
    i2 = jnp.min(jnp.where(em2 == e2, row, 99), axis=0, keepdims=True)
    r = jnp.exp(e2 - e1)
    c1 = grp_w / (1.0 + r)
    c2 = grp_w * r / (1.0 + r)
    sel1 = row == i1
    sel2 = row == i2
    oh = jnp.where(sel1 | sel2, 1.0, 0.0)
    rank = _dot(oh.astype(BF16), ut_ref[...])
    cnt = jnp.sum(oh, axis=1, keepdims=True)
    padded = jnp.ceil(cnt * (1.0 / SEG_ALIGN)) * SEG_ALIGN
    off = _dot_hi(lt_ref[...], jnp.broadcast_to(padded, (N_EXPERTS, 128)))
    pos = off[:, 0:1] + rank
    pos1 = jnp.sum(jnp.where(sel1, pos, 0.0), axis=0, keepdims=True)
    pos2 = jnp.sum(jnp.where(sel2, pos, 0.0), axis=0, keepdims=True)
    info_ref[0] = jnp.concatenate([pos1, pos2, c1, c2, jnp.zeros((4, tb), F32)], axis=0)
    off_ref[0] = off.astype(jnp.int32)
    cnt_ref[0] = jnp.broadcast_to(cnt, (N_EXPERTS, 128)).astype(jnp.int32)


def _outproj(x, o_f, o_b, gg, yf, ot, lw, tabs, tb):
    B, S, D = x.shape
    nsb = S // tb
    tok = lambda w: pl.BlockSpec((1, tb, w), lambda b, i: (b, i, 0))
    weights_a = [lw["gng"], tabs["hm"]]
    weights_b = [lw["wo"], lw["n2g"], lw["wrh"], lw["wrl"], lw["br"], tabs["ut"], tabs["lt"]]
    in_specs = ([tok(D), tok(GLA_W), tok(GLA_W), tok(GLA_W)] + [_full(w.shape) for w in weights_a]
                + [tok(FNET_W), pl.BlockSpec((1, MLA_W, tb), lambda b, i: (b, 0, i))]
                + [_full(w.shape) for w in weights_b])
    blk = lambda r, c, dt: (jax.ShapeDtypeStruct((B * nsb, r, c), dt),
                            pl.BlockSpec((1, r, c), lambda b, i: (b * nsb + i, 0, 0)))
    outs = [(jax.ShapeDtypeStruct((B, S, D), F32), tok(D)), (jax.ShapeDtypeStruct((B, S, D), BF16), tok(D)),
            blk(8, tb, F32), blk(N_EXPERTS, 128, jnp.int32), blk(N_EXPERTS, 128, jnp.int32)]
    return pl.pallas_call(
        _outproj_kernel, grid=(B, nsb), in_specs=in_specs,
        out_specs=[o[1] for o in outs], out_shape=[o[0] for o in outs],
        compiler_params=_params(("parallel", "parallel")), name="outproj")(
            x, o_f, o_b, gg, *weights_a, yf, ot, *weights_b)


def _moe_kernel(off_ref, cnt_ref, h_ref, info_ref, wg_ref, wu_ref, wd_ref, y_ref,
                pm_ref, xs_ref, ow_ref, cs_ref):
    blk = pl.program_id(0)
    e = pl.program_id(1)
    ns, tb = pm_ref.shape

    @pl.when(e == 0)
    def _():
        info = info_ref[0]
        pos1 = info[0:1].astype(jnp.int32)
        pos2 = info[1:2].astype(jnp.int32)
        slot = lax.broadcasted_iota(jnp.int32, (ns, tb), 0)
        p1 = slot == pos1
        p2 = slot == pos2
        pm = jnp.where(p1 | p2, 1.0, 0.0).astype(BF16)
        pm_ref[...] = pm
        cs_ref[...] = jnp.sum(jnp.where(p1, info[2:3], 0.0) + jnp.where(p2, info[3:4], 0.0), axis=1,
                              keepdims=True)
        xs_ref[...] = _dot(pm, h_ref[...]).astype(BF16)
        ow_ref[...] = jnp.zeros_like(ow_ref)

    ia = blk * N_EXPERTS + 2 * e
    off_a, off_b = off_ref[ia], off_ref[ia + 1]
    nt_a = (cnt_ref[ia] + ROW_TILE - 1) // ROW_TILE
    nt_b = (cnt_ref[ia + 1] + ROW_TILE - 1) // ROW_TILE

    def tile(off, k, i):
        rows = pl.ds(pl.multiple_of(off + i * ROW_TILE, SEG_ALIGN), ROW_TILE)
        xt = xs_ref[rows, :]
        g = _dot(xt, wg_ref[k])
        u = _dot(xt, wu_ref[k])
        hh = (g * jax.nn.sigmoid(g) * u).astype(BF16)
        o = _dot(hh, wd_ref[k])
        ow_ref[rows, :] = (o * cs_ref[rows, :]).astype(BF16)

    def loop(off, k, lo, hi):
        lax.fori_loop(lo, hi, lambda i, c: (tile(off, k, i), c)[1], 0)

    @pl.when(nt_a <= 1)
    def _():
        tile(off_a, 0, 0)
        tile(off_b, 1, 0)
        loop(off_b, 1, 1, nt_b)

    @pl.when(nt_a > 1)
    def _():
        loop(off_a, 0, 0, nt_a)
        loop(off_b, 1, 0, nt_b)

    @pl.when(e == N_EXPERTS // 2 - 1)
    def _():
        y_ref[...] = lax.dot_general(pm_ref[...], ow_ref[...], TN_DIMS, preferred_element_type=F32)


def _moe(h, info, off, cnt, lw, tb):
    T, D = h.shape
    nblk = T // tb
    ns = 2 * tb + N_EXPERTS * SEG_ALIGN + ROW_TILE
    grid_spec = pltpu.PrefetchScalarGridSpec(
        num_scalar_prefetch=2, grid=(nblk, N_EXPERTS // 2),
        in_specs=[pl.BlockSpec((tb, D), lambda b, e, *_: (b, 0)),
                  pl.BlockSpec((1, 8, tb), lambda b, e, *_: (b, 0, 0)),
                  pl.BlockSpec((2, D, D_EXPERT), lambda b, e, *_: (e, 0, 0)),
                  pl.BlockSpec((2, D, D_EXPERT), lambda b, e, *_: (e, 0, 0)),
                  pl.BlockSpec((2, D_EXPERT, D), lambda b, e, *_: (e, 0, 0))],
        out_specs=pl.BlockSpec((tb, D), lambda b, e, *_: (b, 0)),
        scratch_shapes=[pltpu.VMEM((ns, tb), BF16), pltpu.VMEM((ns, D), BF16), pltpu.VMEM((ns, D), BF16),
                        pltpu.VMEM((ns, 1), F32)])
    return pl.pallas_call(
        _moe_kernel, grid_spec=grid_spec, out_shape=jax.ShapeDtypeStruct((T, D), F32),
        compiler_params=_params(("parallel", "arbitrary")), name="moe")(
            off, cnt, h, info, lw["wg"], lw["wu"], lw["wd"])


def _final_kernel(x_ref, y_ref, g_ref, o_ref):
    o_ref[...] = _rms(x_ref[...] + y_ref[...], g_ref[...])


def _final(x, y, g, tm):
    T, D = x.shape
    blk = pl.BlockSpec((tm, D), lambda i: (i, 0))
    return pl.pallas_call(
        _final_kernel, grid=(T // tm,), in_specs=[blk, blk, _full((1, D))], out_specs=blk,
        out_shape=jax.ShapeDtypeStruct((T, D), F32),
        compiler_params=_params(("parallel",)), name="final")(x, y, g)


def _block_diag(blocks):
    n = len(blocks)
    r, c = blocks[0].shape
    out = jnp.zeros((n * r, n * c), blocks[0].dtype)
    for i, b in enumerate(blocks):
        out = out.at[i * r:(i + 1) * r, i * c:(i + 1) * c].set(b)
    return out


def _tables(positions, S, tb):
    B = positions.shape[0]
    n1 = 1 << ((S.bit_length() - 1 + 1) // 2)
    n2 = S // n1
    assert n1 * n2 == S
    half = MLA_ROPE // 2
    inv = 1.0 / (ROPE_THETA ** (jnp.arange(0, MLA_ROPE, 2, dtype=F32) / MLA_ROPE))
    ang = positions.astype(F32)[..., None] * inv
    cos, sin = jnp.cos(ang), jnp.sin(ang)
    z = lambda n: jnp.zeros((B, S, n), F32)
    pad = MLA_QK_PAD - MLA_NOPE - MLA_ROPE
    cos_q = jnp.concatenate([jnp.ones((B, S, MLA_NOPE), F32), cos, cos, z(pad)], -1)
    sin_q = jnp.concatenate([z(MLA_NOPE), -sin, sin, z(pad)], -1)
    cos_k = jnp.concatenate([z(MLA_NOPE), cos, cos, z(pad)], -1)
    dft = lambda n: 2.0 * np.pi * np.outer(np.arange(n), np.arange(n)) / n
    a64 = dft(FNET_DG)
    c64 = jnp.asarray(np.cos(a64) / math.sqrt(FNET_DG), F32)
    s64 = jnp.asarray(np.sin(a64) / math.sqrt(FNET_DG), F32)
    a2, a1 = dft(n2), dft(n1)
    c2, s2 = np.cos(a2) / math.sqrt(n2), np.sin(a2) / math.sqrt(n2)
    c1, s1 = np.cos(a1) / math.sqrt(n1), np.sin(a1) / math.sqrt(n1)
    m1 = np.block([[c2, s2], [-s2, c2]])
    m2 = np.concatenate([c1, s1], axis=1)
    kk = jnp.arange(n2, dtype=jnp.int32)[:, None] * jnp.arange(n1, dtype=jnp.int32)[None, :]
    th = (kk % S).astype(F32) * (2.0 * math.pi / S)
    hm = np.kron(np.eye(GLA_HEADS), np.full((GLA_DV, GLA_DV), 1.0 / GLA_DV))
    tt = np.arange(tb)
    return dict(
        n1=n1, n2=n2,
        cos_t=cos_q.transpose(0, 2, 1), sin_t=sin_q.transpose(0, 2, 1), cos_k=cos_k, sin_k=sin_q,
        cbd=_block_diag([c64] * FNET_GROUPS).astype(BF16), sbd=_block_diag([s64] * FNET_GROUPS).astype(BF16),
        m1=jnp.asarray(m1, BF16), m2=jnp.asarray(m2, BF16),
        tw_c=jnp.repeat(jnp.cos(th), FNET_W, axis=1), tw_s=jnp.repeat(jnp.sin(th), FNET_W, axis=1),
        hm=jnp.asarray(hm, BF16),
        ut=jnp.asarray(tt[:, None] < tt[None, :], BF16),
        lt=jnp.asarray(np.arange(N_EXPERTS)[None, :] < np.arange(N_EXPERTS)[:, None], F32))


def _layer_weights(l, p):
    w_in = p["w_in"][l]
    D = w_in.shape[0]
    o = np.cumsum([0, GLA_QK, GLA_QK, GLA_W, GLA_W, GLA_RANK, GLA_RANK, FNET_W, MLA_Q_RANK, MLA_KV_RANK, MLA_ROPE])
    col = lambda i: w_in[:, int(o[i]):int(o[i + 1])]
    zc = lambda n: jnp.zeros((D, n), F32)
    half = MLA_ROPE // 2
    kr = col(9)
    pad = MLA_QK_PAD - MLA_NOPE - MLA_ROPE
    kr1 = jnp.concatenate([zc(MLA_NOPE), kr[:, :half], kr[:, half:], zc(pad)], 1)
    kr2 = jnp.concatenate([zc(MLA_NOPE), kr[:, half:], kr[:, :half], zc(pad)], 1)
    w1 = jnp.concatenate([col(0), col(1), col(2), col(3), col(4), col(5), zc(128 - 2 * GLA_RANK), col(6), col(7),
                          col(8), kr1, kr2], 1).astype(BF16)
    wab = jnp.zeros((128, 2 * GLA_QK), F32)
    wab = wab.at[0:GLA_RANK, 0:GLA_QK].set(p["gla_wa_f"][l])
    wab = wab.at[GLA_RANK:2 * GLA_RANK, GLA_QK:].set(p["gla_wa_b"][l])
    bab = jnp.concatenate([p["gla_ba_f"][l], p["gla_ba_b"][l]])[None, :]
    wuq = p["mla_w_uq"][l].reshape(MLA_Q_RANK, MLA_HEADS, MLA_NOPE + MLA_ROPE)
    zq = lambda n: jnp.zeros((MLA_Q_RANK, MLA_HEADS, n), F32)
    nope, x1, x2 = wuq[..., :MLA_NOPE], wuq[..., MLA_NOPE:MLA_NOPE + half], wuq[..., MLA_NOPE + half:]
    wq1 = jnp.concatenate([nope, x1, x2, zq(pad)], -1).reshape(MLA_Q_RANK, -1)
    wq2 = jnp.concatenate([zq(MLA_NOPE), x2, x1, zq(pad)], -1).reshape(MLA_Q_RANK, -1)
    wukv = p["mla_w_ukv"][l].reshape(MLA_KV_RANK, MLA_HEADS, MLA_NOPE + MLA_V)
    wk = jnp.concatenate([wukv[..., :MLA_NOPE], jnp.zeros((MLA_KV_RANK, MLA_HEADS, MLA_QK_PAD - MLA_NOPE), F32)],
                         -1).reshape(MLA_KV_RANK, -1)
    wvt = wukv[..., MLA_NOPE:].reshape(MLA_KV_RANK, -1).T
    wr = jnp.zeros((256, D), F32)
    wr = wr.at[0:N_EXPERTS].set(p["moe_w_re"][l].T).at[128:128 + N_GROUPS].set(p["moe_w_rg"][l].T)
    br = jnp.zeros((256, 1), F32)
    br = br.at[0:N_EXPERTS, 0].set(p["moe_b_re"][l]).at[128:128 + N_GROUPS, 0].set(p["moe_b_rg"][l])
    wrh = wr.astype(BF16)
    return dict(
        g1=p["norm1_g"][l][None, :], w1=w1, wab=wab.astype(BF16), bab=bab,
        wmix=_block_diag([p["fnet_w"][l, g] for g in range(FNET_GROUPS)]).astype(BF16),
        qg=p["mla_q_norm_g"][l][None, :], wq1=wq1.T.astype(BF16), wq2=wq2.T.astype(BF16),
        kvg=p["mla_kv_norm_g"][l][None, :], wk=wk.astype(BF16), wvt=wvt.astype(BF16),
        gng=jnp.tile(p["gla_norm_g"][l], GLA_HEADS)[None, :], wo=p["w_out"][l].astype(BF16),
        n2g=p["norm2_g"][l][None, :], wrh=wrh, wrl=(wr - wrh.astype(F32)).astype(BF16), br=br,
        wg=p["moe_w_gate"][l].astype(BF16), wu=p["moe_w_up"][l].astype(BF16), wd=p["moe_w_down"][l].astype(BF16))


def _tiles(S):
    return dict(tm=min(512, S), tg=min(512, S), ts1=8, tk2=8, tq=min(512, S), tk=min(1024, S // 2), tb=min(1024, S))


def kernel(x, positions, norm1_g, w_in, gla_wa_f, gla_ba_f, gla_wa_b, gla_ba_b, gla_norm_g, fnet_w, mla_q_norm_g,
           mla_w_uq, mla_kv_norm_g, mla_w_ukv, w_out, norm2_g, moe_w_rg, moe_b_rg, moe_w_re, moe_b_re, moe_w_gate,
           moe_w_up, moe_w_down, final_norm_g):
    p = dict(norm1_g=norm1_g, w_in=w_in, gla_wa_f=gla_wa_f, gla_ba_f=gla_ba_f, gla_wa_b=gla_wa_b,
             gla_ba_b=gla_ba_b, gla_norm_g=gla_norm_g, fnet_w=fnet_w, mla_q_norm_g=mla_q_norm_g,
             mla_w_uq=mla_w_uq, mla_kv_norm_g=mla_kv_norm_g, mla_w_ukv=mla_w_ukv, w_out=w_out, norm2_g=norm2_g,
             moe_w_rg=moe_w_rg, moe_b_rg=moe_b_rg, moe_w_re=moe_w_re, moe_b_re=moe_b_re, moe_w_gate=moe_w_gate,
             moe_w_up=moe_w_up, moe_w_down=moe_w_down)
    B, S, D = x.shape
    t = _tiles(S)
    tabs = _tables(positions, S, t["tb"])
    y = None
    for l in range(w_in.shape[0]):
        lw = _layer_weights(l, p)
        x, (gq, gk, gv, gg, la_f, la_b, zr, zi, qt, kk, vt) = _proj(x, y, lw, tabs, t["tm"])
        o_f, o_b = _gla(gq, gk, gv, la_f, la_b, t["tg"])
        yf = _fnet(zr, zi, tabs, min(t["ts1"], tabs["n1"]), min(t["tk2"], tabs["n2"]))
        ot = _attn(qt, kk, vt, t["tq"], t["tk"])
        x, h, info, off, cnt = _outproj(x, o_f, o_b, gg, yf, ot, lw, tabs, t["tb"])
        y = _moe(h.reshape(B * S, D), info, off[:, :, 0].reshape(-1), cnt[:, :, 0].reshape(-1), lw, t["tb"])
        y = y.reshape(B, S, D)
    return _final(x.reshape(B * S, D), y.reshape(B * S, D), final_norm_g[None, :], t["tm"]).reshape(B, S, D)
```

```python
import functools
import math

import numpy as np
import jax
import jax.numpy as jnp
from jax import lax
from jax.experimental import pallas as pl
from jax.experimental.pallas import tpu as pltpu

F32 = jnp.float32
BF16 = jnp.bfloat16
EPS = 1e-6

GLA_HEADS, GLA_DK, GLA_DV, GLA_QK, GLA_W, GLA_RANK = 4, 32, 64, 128, 256, 16
GLA_TAU, GLA_CHUNK = 16.0, 64
FNET_GROUPS, FNET_DG, FNET_W = 4, 64, 256
MLA_HEADS, MLA_NOPE, MLA_ROPE, MLA_V = 8, 64, 32, 64
MLA_Q_RANK, MLA_KV_RANK, MLA_W = 256, 128, 512
MLA_QK_PAD = 128
MLA_V_ROWS = 80
ATTN_CHUNK = 512
ATTN_ROTATIONS = 1
ROPE_THETA = 10000.0
N_GROUPS, EXPERTS_PER_GROUP, N_EXPERTS, D_EXPERT = 4, 8, 32, 256

SEG_ALIGN = 16
ROW_TILE = 128
VMEM_LIMIT = 56 * 1024 * 1024

NT_DIMS = (((1,), (1,)), ((), ()))
TN_DIMS = (((0,), (0,)), ((), ()))


def _dot(a, b):
    return jnp.dot(a, b, preferred_element_type=F32)


def _dot_hi(a, b):
    return jnp.dot(a, b, preferred_element_type=F32, precision=lax.Precision.HIGHEST)


def _rms(x, g):
    return x * lax.rsqrt(jnp.mean(x * x, axis=-1, keepdims=True) + EPS) * g


def _params(sem):
    return pltpu.CompilerParams(dimension_semantics=sem, vmem_limit_bytes=VMEM_LIMIT)


def _full(shape):
    return pl.BlockSpec(shape, lambda *_: (0,) * len(shape))


def _proj_kernel(x_ref, g1_ref, w1_ref, wab_ref, bab_ref, cbd_ref, sbd_ref, wmix_ref, qg_ref, wq1_ref, wq2_ref,
                 cost_ref, sint_ref, kvg_ref, wk_ref, cosk_ref, sink_ref, wvt_ref,
                 gq_ref, gk_ref, gv_ref, gg_ref, laf_ref, lab_ref, zr_ref, zi_ref, qt_ref, kk_ref, vt_ref):
    xn = _rms(x_ref[0], g1_ref[...]).astype(BF16)
    p = _dot(xn, w1_ref[...])
    gq_ref[0] = (p[:, 0:128] * (GLA_DK ** -0.5)).astype(BF16)
    gk_ref[0] = p[:, 128:256].astype(BF16)
    gv_ref[0] = p[:, 256:512].astype(BF16)
    gg_ref[0] = p[:, 512:768].astype(BF16)
    pre = _dot(p[:, 768:896].astype(BF16), wab_ref[...]) + bab_ref[...]
    la = (jnp.minimum(pre, 0.0) - jnp.log(1.0 + jnp.exp(-jnp.abs(pre)))) * (1.0 / GLA_TAU)
    laf_ref[0] = la[:, 0:128]
    lab_ref[0] = la[:, 128:256]
    u = p[:, 896:1152].astype(BF16)
    zr_ref[0] = _dot(_dot(u, cbd_ref[...]).astype(BF16), wmix_ref[...]).astype(BF16)
    zi_ref[0] = (-_dot(_dot(u, sbd_ref[...]).astype(BF16), wmix_ref[...])).astype(BF16)
    cqn = _rms(p[:, 1152:1408], qg_ref[...]).astype(BF16)
    qa = lax.dot_general(wq1_ref[...], cqn, NT_DIMS, preferred_element_type=F32)
    qb = lax.dot_general(wq2_ref[...], cqn, NT_DIMS, preferred_element_type=F32)
    cos_t = cost_ref[0]
    sin_t = sint_ref[0]
    scale = (MLA_NOPE + MLA_ROPE) ** -0.5 * math.log2(math.e)
    for h in range(MLA_HEADS):
        sl = slice(h * MLA_QK_PAD, (h + 1) * MLA_QK_PAD)
        qt_ref[0, h] = ((qa[sl] * cos_t + qb[sl] * sin_t) * scale).astype(BF16)
    kvn = _rms(p[:, 1408:1536], kvg_ref[...]).astype(BF16)
    kf = _dot(kvn, wk_ref[...])
    krope = p[:, 1536:1664] * cosk_ref[0] + p[:, 1664:1792] * sink_ref[0]
    for h in range(MLA_HEADS):
        sl = slice(h * MLA_QK_PAD, (h + 1) * MLA_QK_PAD)
        kk_ref[0, h] = (kf[:, sl] + krope).astype(BF16)
    vt = lax.dot_general(wvt_ref[...], kvn, NT_DIMS, preferred_element_type=F32)
    tm = vt.shape[1]
    ones = jnp.ones((MLA_V_ROWS - MLA_V, tm), BF16)
    for h in range(MLA_HEADS):
        vt_ref[0, h, 0:MLA_V, :] = vt[h * MLA_V:(h + 1) * MLA_V].astype(BF16)
        vt_ref[0, h, MLA_V:MLA_V_ROWS, :] = ones


def _proj(x, lw, tabs, tm):
    B, S, D = x.shape
    H = MLA_HEADS
    tok = lambda w: pl.BlockSpec((1, tm, w), lambda b, i: (b, i, 0))
    in_specs = [tok(D)]
    weights = [lw["g1"], lw["w1"], lw["wab"], lw["bab"], tabs["cbd"], tabs["sbd"], lw["wmix"], lw["qg"],
               lw["wq1"], lw["wq2"]]
    in_specs += [_full(w.shape) for w in weights]
    in_specs += [pl.BlockSpec((1, MLA_QK_PAD, tm), lambda b, i: (b, 0, i))] * 2
    in_specs += [_full(lw["kvg"].shape), _full(lw["wk"].shape)]
    in_specs += [tok(MLA_QK_PAD)] * 2
    in_specs += [_full(lw["wvt"].shape)]
    args = [x] + weights + [tabs["cos_t"], tabs["sin_t"], lw["kvg"], lw["wk"], tabs["cos_k"], tabs["sin_k"],
                            lw["wvt"]]
    out_shape, out_specs = [], []
    for w, dt in ((128, BF16), (128, BF16), (256, BF16), (256, BF16), (128, F32), (128, F32), (256, BF16),
                  (256, BF16)):
        out_shape.append(jax.ShapeDtypeStruct((B, S, w), dt))
        out_specs.append(tok(w))
    out_shape.append(jax.ShapeDtypeStruct((B, H, MLA_QK_PAD, S), BF16))
    out_specs.append(pl.BlockSpec((1, H, MLA_QK_PAD, tm), lambda b, i: (b, 0, 0, i)))
    out_shape.append(jax.ShapeDtypeStruct((B, H, S, MLA_QK_PAD), BF16))
    out_specs.append(pl.BlockSpec((1, H, tm, MLA_QK_PAD), lambda b, i: (b, 0, i, 0)))
    out_shape.append(jax.ShapeDtypeStruct((B, H, MLA_V_ROWS, S), BF16))
    out_specs.append(pl.BlockSpec((1, H, MLA_V_ROWS, tm), lambda b, i: (b, 0, 0, i)))
    return pl.pallas_call(
        _proj_kernel, grid=(B, S // tm), in_specs=in_specs, out_specs=out_specs, out_shape=out_shape,
        compiler_params=_params(("parallel", "parallel")), name="proj")(*args)


def _gla_consts():
    C, W, QK = GLA_CHUNK, GLA_W, GLA_QK
    i64 = lax.broadcasted_iota(jnp.int32, (C, C), 0)
    j64 = lax.broadcasted_iota(jnp.int32, (C, C), 1)
    r_att = lax.broadcasted_iota(jnp.int32, (C, W), 0)
    c_att = lax.broadcasted_iota(jnp.int32, (C, W), 1) % C
    r_ks = lax.broadcasted_iota(jnp.int32, (W, QK), 0) // C
    c_ks = lax.broadcasted_iota(jnp.int32, (W, QK), 1) // GLA_DK
    r_bd = lax.broadcasted_iota(jnp.int32, (W, W), 0) // C
    c_bd = lax.broadcasted_iota(jnp.int32, (W, W), 1) // GLA_DV
    r_s = lax.broadcasted_iota(jnp.int32, (QK, W), 0) // GLA_DK
    c_s = lax.broadcasted_iota(jnp.int32, (QK, W), 1) // GLA_DV
    return dict(
        tri_f=(j64 <= i64).astype(F32), tri_b=(j64 >= i64).astype(F32),
        att_f=c_att <= r_att, att_b=c_att > r_att,
        ks=r_ks == c_ks, bd=r_bd == c_bd, st=r_s == c_s)


def _gla_chunk(q, k, v, la, state, fwd, cs):
    C = GLA_CHUNK
    b = _dot_hi(cs["tri_f"] if fwd else cs["tri_b"], la)
    b_edge = b[C - 1:C] if fwd else b[0:1]
    qd = (q * jnp.exp(b)).astype(BF16)
    ki = (k * jnp.exp(-b)).astype(BF16)
    ke = (k * jnp.exp(b_edge - b)).astype(BF16)
    kstack = jnp.where(cs["ks"], jnp.concatenate([ki] * GLA_HEADS, axis=0), jnp.zeros((), BF16))
    att = lax.dot_general(qd, kstack, NT_DIMS, preferred_element_type=F32)
    att = jnp.where(cs["att_f"] if fwd else cs["att_b"], att, 0.0).astype(BF16)
    vbd = jnp.where(cs["bd"], jnp.concatenate([v] * GLA_HEADS, axis=0), jnp.zeros((), BF16))
    o = _dot(att, vbd) + _dot(qd, state.astype(BF16))
    kv = lax.dot_general(ke, v, TN_DIMS, preferred_element_type=F32)
    dec = jnp.exp(jnp.broadcast_to(b_edge, (GLA_QK, GLA_QK)).T)
    state = jnp.concatenate([dec, dec], axis=1) * state + jnp.where(cs["st"], kv, 0.0)
    return o, state


def _gla_kernel(qf_ref, kf_ref, vf_ref, laf_ref, qb_ref, kb_ref, vb_ref, lab_ref, of_ref, ob_ref,
                sf_ref, sb_ref, *, nchunk):
    C = GLA_CHUNK
    nbatch = sf_ref.shape[0]

    @pl.when(pl.program_id(0) == 0)
    def _():
        sf_ref[...] = jnp.zeros_like(sf_ref)
        sb_ref[...] = jnp.zeros_like(sb_ref)

    cs = _gla_consts()

    def body(c, carry):
        s_f, s_b = list(carry[0]), list(carry[1])
        rf = pl.ds(pl.multiple_of(c * C, C), C)
        rb = pl.ds(pl.multiple_of((nchunk - 1 - c) * C, C), C)
        for b in range(nbatch):
            o, s_f[b] = _gla_chunk(qf_ref[b, rf, :].astype(F32), kf_ref[b, rf, :].astype(F32), vf_ref[b, rf, :],
                                   laf_ref[b, rf, :], s_f[b], True, cs)
            of_ref[b, rf, :] = o
            o, s_b[b] = _gla_chunk(qb_ref[b, rb, :].astype(F32), kb_ref[b, rb, :].astype(F32), vb_ref[b, rb, :],
                                   lab_ref[b, rb, :], s_b[b], False, cs)
            ob_ref[b, rb, :] = o
        return tuple(s_f), tuple(s_b)

    init = (tuple(sf_ref[b] for b in range(nbatch)), tuple(sb_ref[b] for b in range(nbatch)))
    s_f, s_b = lax.fori_loop(0, nchunk, body, init)
    for b in range(nbatch):
        sf_ref[b] = s_f[b]
        sb_ref[b] = s_b[b]


def _gla(gq, gk, gv, la_f, la_b, tg):
    B, S, _ = gq.shape
    nb = S // tg
    fw = lambda w: pl.BlockSpec((B, tg, w), lambda i: (0, i, 0))
    bw = lambda w: pl.BlockSpec((B, tg, w), lambda i: (0, nb - 1 - i, 0))
    return pl.pallas_call(
        functools.partial(_gla_kernel, nchunk=tg // GLA_CHUNK),
        grid=(nb,),
        in_specs=[fw(128), fw(128), fw(256), fw(128), bw(128), bw(128), bw(256), bw(128)],
        out_specs=[fw(256), bw(256)],
        out_shape=[jax.ShapeDtypeStruct((B, S, GLA_W), F32)] * 2,
        scratch_shapes=[pltpu.VMEM((B, GLA_QK, GLA_W), F32)] * 2,
        compiler_params=_params(("arbitrary",)), name="gla")(gq, gk, gv, la_f, gq, gk, gv, la_b)


def _fnet1_kernel(zr_ref, zi_ref, m1_ref, tc_ref, ts_ref, or_ref, oi_ref, *, ts1):
    n2 = zr_ref.shape[1]
    x = jnp.concatenate([zr_ref[0], zi_ref[0]], axis=0)
    z = _dot(m1_ref[...], x)
    zr, zi = z[:n2], z[n2:]
    tc, tsn = tc_ref[...], ts_ref[...]
    pr = zr * tc + zi * tsn
    pi = zi * tc - zr * tsn
    for j in range(ts1):
        sl = slice(j * FNET_W, (j + 1) * FNET_W)
        or_ref[0, j] = pr[:, sl].astype(BF16)
        oi_ref[0, j] = pi[:, sl].astype(BF16)


def _fnet2_kernel(zr_ref, zi_ref, m2_ref, o_ref):
    x = jnp.concatenate([zr_ref[0], zi_ref[0]], axis=0)
    o_ref[0] = _dot(m2_ref[...], x).astype(BF16)


def _fnet(zr, zi, tabs, ts1, tk2):
    B, S, CH = zr.shape
    n1, n2 = tabs["n1"], tabs["n2"]
    zr2 = zr.reshape(B, n2, n1 * CH)
    zi2 = zi.reshape(B, n2, n1 * CH)
    cols = ts1 * CH
    in_b = pl.BlockSpec((1, n2, cols), lambda b, i: (b, 0, i))
    tw_b = pl.BlockSpec((n2, cols), lambda b, i: (0, i))
    out_b = pl.BlockSpec((1, ts1, n2, CH), lambda b, i: (b, i, 0, 0))
    pr, pi = pl.pallas_call(
        functools.partial(_fnet1_kernel, ts1=ts1),
        grid=(B, n1 // ts1),
        in_specs=[in_b, in_b, _full(tabs["m1"].shape), tw_b, tw_b],
        out_specs=[out_b, out_b],
        out_shape=[jax.ShapeDtypeStruct((B, n1, n2, CH), BF16)] * 2,
        compiler_params=_params(("parallel", "parallel")), name="fnet1")(zr2, zi2, tabs["m1"], tabs["tw_c"],
                                                                         tabs["tw_s"])
    pr = pr.reshape(B, n1, n2 * CH)
    pi = pi.reshape(B, n1, n2 * CH)
    cols2 = tk2 * CH
    blk = pl.BlockSpec((1, n1, cols2), lambda b, i: (b, 0, i))
    out = pl.pallas_call(
        _fnet2_kernel, grid=(B, n2 // tk2),
        in_specs=[blk, blk, _full(tabs["m2"].shape)], out_specs=blk,
        out_shape=jax.ShapeDtypeStruct((B, n1, n2 * CH), BF16),
        compiler_params=_params(("parallel", "parallel")), name="fnet2")(pr, pi, tabs["m2"])
    return out.reshape(B, S, CH)


def _attn_kernel(qt_ref, k_ref, vt_ref, o_ref, s0_ref, s1_ref, s2_ref, *, tk, nk):
    qt = qt_ref[0, 0]
    tq = qt.shape[1]

    nc = tk // ATTN_CHUNK
    neg = jnp.full((1, tq), -1e30, F32)

    def step(j_new, s_new, j_cur, s_cur, mt_cur, m, acc):
        mt_new = neg
        if s_cur is not None:
            m_new = jnp.maximum(m, mt_cur)
            acc = jnp.exp2(m - m_new) * acc
            m = m_new
        for c in range(nc):
            rc = slice(c * ATTN_CHUNK, (c + 1) * ATTN_CHUNK)
            if s_new is not None:
                ks = pl.ds(pl.multiple_of(j_new * tk + c * ATTN_CHUNK, ATTN_CHUNK), ATTN_CHUNK)
                s = _dot(k_ref[0, 0, ks, :], qt)
                s_new[rc, :] = s
                mt_new = jnp.maximum(mt_new, jnp.max(s, axis=0, keepdims=True))
            if s_cur is not None:
                ks = pl.ds(pl.multiple_of(j_cur * tk + c * ATTN_CHUNK, ATTN_CHUNK), ATTN_CHUNK)
                p = jnp.exp2((s_cur[rc, :] - m).astype(BF16))
                acc = acc + _dot(vt_ref[0, 0, :, ks], p)
        return mt_new, m, acc

    bufs = (s0_ref, s1_ref, s2_ref)

    def body(i, carry):
        m, acc, mta, mtb = carry
        for r in range(ATTN_ROTATIONS):
            t = 3 * (ATTN_ROTATIONS * i + r)
            mtc, m, acc = step(t + 2, bufs[2], t, bufs[0], mta, m, acc)
            mta, m, acc = step(t + 3, bufs[0], t + 1, bufs[1], mtb, m, acc)
            mtb, m, acc = step(t + 4, bufs[1], t + 2, bufs[2], mtc, m, acc)
        return m, acc, mta, mtb

    acc0 = jnp.zeros((MLA_V_ROWS, tq), F32)
    mta, _, _ = step(0, bufs[0], None, None, None, None, None)
    mtb, _, _ = step(1, bufs[1], None, None, None, None, None)
    nloop = (nk - 2) // (3 * ATTN_ROTATIONS)
    m, acc, mta, mtb = lax.fori_loop(0, nloop, body, (neg, acc0, mta, mtb))
    done = 3 * ATTN_ROTATIONS * nloop
    mts = {done: mta, done + 1: mtb}
    for t in range(done, nk):
        new = t + 2 if t + 2 < nk else None
        mt_new, m, acc = step(new, None if new is None else bufs[new % 3], t, bufs[t % 3], mts[t], m, acc)
        mts[t + 2] = mt_new
    o_ref[0] = (acc[0:MLA_V] / acc[MLA_V:MLA_V + 1]).astype(BF16)


def _attn(qt, kk, vt, tq, tk):
    B, H, _, S = qt.shape
    assert S // tk >= 2 and tk % ATTN_CHUNK == 0
    return pl.pallas_call(
        functools.partial(_attn_kernel, tk=tk, nk=S // tk),
        grid=(B, H, S // tq),
        in_specs=[pl.BlockSpec((1, 1, MLA_QK_PAD, tq), lambda b, h, i: (b, h, 0, i)),
                  pl.BlockSpec((1, 1, S, MLA_QK_PAD), lambda b, h, i: (b, h, 0, 0)),
                  pl.BlockSpec((1, 1, MLA_V_ROWS, S), lambda b, h, i: (b, h, 0, 0))],
        out_specs=pl.BlockSpec((1, MLA_V, tq), lambda b, h, i: (b, h, i)),
        out_shape=jax.ShapeDtypeStruct((B, H * MLA_V, S), BF16),
        scratch_shapes=[pltpu.VMEM((tk, tq), F32)] * 3,
        compiler_params=_params(("parallel", "parallel", "arbitrary")), name="attn")(qt, kk, vt)


def _outproj_kernel(x_ref, of_ref, ob_ref, gg_ref, gng_ref, hm_ref, yf_ref, ot_ref, wo_ref, n2g_ref,
                    wrh_ref, wrl_ref, br_ref, ut_ref, lt_ref,
                    x1_ref, h_ref, info_ref, off_ref, cnt_ref):
    o = of_ref[0] + ob_ref[0]
    msq = _dot((o * o).astype(BF16), hm_ref[...])
    gate = gg_ref[0].astype(F32)
    og = o * lax.rsqrt(msq + EPS) * gng_ref[...] * (gate * jax.nn.sigmoid(gate))
    acc = _dot(og.astype(BF16), wo_ref[0:GLA_W, :])
    acc += _dot(yf_ref[0], wo_ref[GLA_W:GLA_W + FNET_W, :])
    acc += lax.dot_general(ot_ref[0], wo_ref[GLA_W + FNET_W:, :], TN_DIMS, preferred_element_type=F32)
    x1 = x_ref[0] + acc
    x1_ref[0] = x1
    h = _rms(x1, n2g_ref[...])
    h_hi = h.astype(BF16)
    h_ref[0] = h_hi
    h_lo = (h - h_hi.astype(F32)).astype(BF16)
    nt = lambda w, a: lax.dot_general(w, a, NT_DIMS, preferred_element_type=F32)
    lg = nt(wrh_ref[...], h_hi) + nt(wrh_ref[...], h_lo) + nt(wrl_ref[...], h_hi) + br_ref[...]
    tb = lg.shape[1]
    neg = jnp.float32(-jnp.inf)
    row8 = lax.broadcasted_iota(jnp.int32, (8, tb), 0)
    g8 = lg[128:136]
    gm = jnp.where(row8 < N_GROUPS, g8, neg)
    gmax = jnp.max(gm, axis=0, keepdims=True)
    gidx = jnp.min(jnp.where(gm == gmax, row8, 99), axis=0, keepdims=True)
    gsum = jnp.sum(jnp.where(row8 < N_GROUPS, jnp.exp(g8 - gmax), 0.0), axis=0, keepdims=True)
    grp_w = 1.0 / gsum
    row = lax.broadcasted_iota(jnp.int32, (N_EXPERTS, tb), 0)
    em = jnp.where((row // EXPERTS_PER_GROUP) == gidx, lg[0:N_EXPERTS], neg)
    e1 = jnp.max(em, axis=0, keepdims=True)
    i1 = jnp.min(jnp.where(em == e1, row, 99), axis=0, keepdims=True)
    em2 = jnp.where(row == i1, neg, em)
    e2 = jnp.max(em2, axis=0, keepdims=True)
    i2 = jnp.min(jnp.where(em2 == e2, row, 99), axis=0, keepdims=True)
    r = jnp.exp(e2 - e1)
    c1 = grp_w / (1.0 + r)
    c2 = grp_w * r / (1.0 + r)
    sel1 = row == i1
    sel2 = row == i2
    oh = jnp.where(sel1 | sel2, 1.0, 0.0)
    rank = _dot(oh.astype(BF16), ut_ref[...])
    cnt = jnp.sum(oh, axis=1, keepdims=True)
    padded = jnp.ceil(cnt * (1.0 / SEG_ALIGN)) * SEG_ALIGN
    off = _dot_hi(lt_ref[...], jnp.broadcast_to(padded, (N_EXPERTS, 128)))
    pos = off[:, 0:1] + rank
    pos1 = jnp.sum(jnp.where(sel1, pos, 0.0), axis=0, keepdims=True)
    pos2 = jnp.sum(jnp.where(sel2, pos, 0.0), axis=0, keepdims=True)
    info_ref[0] = jnp.concatenate([pos1, pos2, c1, c2, jnp.zeros((4, tb), F32)], axis=0)
    off_ref[0] = off.astype(jnp.int32)
    cnt_ref[0] = jnp.broadcast_to(cnt, (N_EXPERTS, 128)).astype(jnp.int32)


def _outproj(x, o_f, o_b, gg, yf, ot, lw, tabs, tb):
    B, S, D = x.shape
    nsb = S // tb
    tok = lambda w: pl.BlockSpec((1, tb, w), lambda b, i: (b, i, 0))
    weights_a = [lw["gng"], tabs["hm"]]
    weights_b = [lw["wo"], lw["n2g"], lw["wrh"], lw["wrl"], lw["br"], tabs["ut"], tabs["lt"]]
    in_specs = ([tok(D), tok(GLA_W), tok(GLA_W), tok(GLA_W)] + [_full(w.shape) for w in weights_a]
                + [tok(FNET_W), pl.BlockSpec((1, MLA_W, tb), lambda b, i: (b, 0, i))]
                + [_full(w.shape) for w in weights_b])
    blk = lambda r, c, dt: (jax.ShapeDtypeStruct((B * nsb, r, c), dt),
                            pl.BlockSpec((1, r, c), lambda b, i: (b * nsb + i, 0, 0)))
    outs = [(jax.ShapeDtypeStruct((B, S, D), F32), tok(D)), (jax.ShapeDtypeStruct((B, S, D), BF16), tok(D)),
            blk(8, tb, F32), blk(N_EXPERTS, 128, jnp.int32), blk(N_EXPERTS, 128, jnp.int32)]
    return pl.pallas_call(
        _outproj_kernel, grid=(B, nsb), in_specs=in_specs,
        out_specs=[o[1] for o in outs], out_shape=[o[0] for o in outs],
        compiler_params=_params(("parallel", "parallel")), name="outproj")(
            x, o_f, o_b, gg, *weights_a, yf, ot, *weights_b)


def _moe_kernel(final_norm, off_ref, cnt_ref, h_ref, info_ref, wg_ref, wu_ref, wd_ref, x1_ref, fg_ref, y_ref,
                pm_ref, xs_ref, ow_ref, cs_ref):
    blk = pl.program_id(0)
    e = pl.program_id(1)
    ns, tb = pm_ref.shape

    @pl.when(e == 0)
    def _():
        info = info_ref[0]
        pos1 = info[0:1].astype(jnp.int32)
        pos2 = info[1:2].astype(jnp.int32)
        rt = max(d for d in range(SEG_ALIGN, 513, SEG_ALIGN) if ns % d == 0)
        for r0 in range(0, ns, rt):
            rows = slice(r0, r0 + rt)
            slot = lax.broadcasted_iota(jnp.int32, (rt, tb), 0) + r0
            p1 = slot == pos1
            p2 = slot == pos2
            pm = jnp.where(p1 | p2, 1.0, 0.0).astype(BF16)
            pm_ref[rows, :] = pm
            cs_ref[rows, :] = jnp.sum(jnp.where(p1, info[2:3], 0.0) + jnp.where(p2, info[3:4], 0.0), axis=1,
                                      keepdims=True)
            xs_ref[rows, :] = _dot(pm, h_ref[...]).astype(BF16)
        ow_ref[...] = jnp.zeros_like(ow_ref)

    ia = blk * N_EXPERTS + 2 * e
    off_a, off_b = off_ref[ia], off_ref[ia + 1]
    nt_a = (cnt_ref[ia] + ROW_TILE - 1) // ROW_TILE
    nt_b = (cnt_ref[ia + 1] + ROW_TILE - 1) // ROW_TILE

    def tile(off, k, i):
        rows = pl.ds(pl.multiple_of(off + i * ROW_TILE, SEG_ALIGN), ROW_TILE)
        xt = xs_ref[rows, :]
        g = _dot(xt, wg_ref[k])
        u = _dot(xt, wu_ref[k])
        hh = (g * jax.nn.sigmoid(g) * u).astype(BF16)
        o = _dot(hh, wd_ref[k])
        ow_ref[rows, :] = (o * cs_ref[rows, :]).astype(BF16)

    def loop(off, k, lo, hi):
        lax.fori_loop(lo, hi, lambda i, c: (tile(off, k, i), c)[1], 0)

    @pl.when(nt_a <= 1)
    def _():
        tile(off_a, 0, 0)
        tile(off_b, 1, 0)
        loop(off_b, 1, 1, nt_b)

    @pl.when(nt_a > 1)
    def _():
        loop(off_a, 0, 0, nt_a)
        loop(off_b, 1, 0, nt_b)

    @pl.when(e == N_EXPERTS // 2 - 1)
    def _():
        x2 = x1_ref[...] + lax.dot_general(pm_ref[...], ow_ref[...], TN_DIMS, preferred_element_type=F32)
        y_ref[...] = _rms(x2, fg_ref[...]) if final_norm else x2


def _moe(h, info, off, cnt, lw, tb, x1, final_g):
    T, D = h.shape
    final_norm = final_g is not None
    fg = final_g if final_norm else jnp.ones((1, D), F32)
    nblk = T // tb
    ns = 2 * tb + N_EXPERTS * SEG_ALIGN + ROW_TILE
    grid_spec = pltpu.PrefetchScalarGridSpec(
        num_scalar_prefetch=2, grid=(nblk, N_EXPERTS // 2),
        in_specs=[pl.BlockSpec((tb, D), lambda b, e, *_: (b, 0)),
                  pl.BlockSpec((1, 8, tb), lambda b, e, *_: (b, 0, 0)),
                  pl.BlockSpec((2, D, D_EXPERT), lambda b, e, *_: (e, 0, 0)),
                  pl.BlockSpec((2, D, D_EXPERT), lambda b, e, *_: (e, 0, 0)),
                  pl.BlockSpec((2, D_EXPERT, D), lambda b, e, *_: (e, 0, 0)),
                  pl.BlockSpec((tb, D), lambda b, e, *_: (b, 0)),
                  pl.BlockSpec((1, D), lambda b, e, *_: (0, 0))],
        out_specs=pl.BlockSpec((tb, D), lambda b, e, *_: (b, 0)),
        scratch_shapes=[pltpu.VMEM((ns, tb), BF16), pltpu.VMEM((ns, D), BF16), pltpu.VMEM((ns, D), BF16),
                        pltpu.VMEM((ns, 1), F32)])
    return pl.pallas_call(
        functools.partial(_moe_kernel, final_norm), grid_spec=grid_spec,
        out_shape=jax.ShapeDtypeStruct((T, D), F32),
        compiler_params=_params(("parallel", "arbitrary")), name="moe")(
            off, cnt, h, info, lw["wg"], lw["wu"], lw["wd"], x1, fg)


def _block_diag(blocks):
    n = len(blocks)
    r, c = blocks[0].shape
    out = jnp.zeros((n * r, n * c), blocks[0].dtype)
    for i, b in enumerate(blocks):
        out = out.at[i * r:(i + 1) * r, i * c:(i + 1) * c].set(b)
    return out


def _tables(positions, S, tb):
    B = positions.shape[0]
    n1 = 1 << ((S.bit_length() - 1 + 1) // 2)
    n2 = S // n1
    assert n1 * n2 == S
    half = MLA_ROPE // 2
    inv = 1.0 / (ROPE_THETA ** (jnp.arange(0, MLA_ROPE, 2, dtype=F32) / MLA_ROPE))
    ang = positions.astype(F32)[..., None] * inv
    cos, sin = jnp.cos(ang), jnp.sin(ang)
    z = lambda n: jnp.zeros((B, S, n), F32)
    pad = MLA_QK_PAD - MLA_NOPE - MLA_ROPE
    cos_q = jnp.concatenate([jnp.ones((B, S, MLA_NOPE), F32), cos, cos, z(pad)], -1)
    sin_q = jnp.concatenate([z(MLA_NOPE), -sin, sin, z(pad)], -1)
    cos_k = jnp.concatenate([z(MLA_NOPE), cos, cos, z(pad)], -1)
    dft = lambda n: 2.0 * np.pi * np.outer(np.arange(n), np.arange(n)) / n
    a64 = dft(FNET_DG)
    c64 = jnp.asarray(np.cos(a64) / math.sqrt(FNET_DG), F32)
    s64 = jnp.asarray(np.sin(a64) / math.sqrt(FNET_DG), F32)
    a2, a1 = dft(n2), dft(n1)
    c2, s2 = np.cos(a2) / math.sqrt(n2), np.sin(a2) / math.sqrt(n2)
    c1, s1 = np.cos(a1) / math.sqrt(n1), np.sin(a1) / math.sqrt(n1)
    m1 = np.block([[c2, s2], [-s2, c2]])
    m2 = np.concatenate([c1, s1], axis=1)
    kk = jnp.arange(n2, dtype=jnp.int32)[:, None] * jnp.arange(n1, dtype=jnp.int32)[None, :]
    th = (kk % S).astype(F32) * (2.0 * math.pi / S)
    hm = np.kron(np.eye(GLA_HEADS), np.full((GLA_DV, GLA_DV), 1.0 / GLA_DV))
    tt = np.arange(tb)
    return dict(
        n1=n1, n2=n2,
        cos_t=cos_q.transpose(0, 2, 1), sin_t=sin_q.transpose(0, 2, 1), cos_k=cos_k, sin_k=sin_q,
        cbd=_block_diag([c64] * FNET_GROUPS).astype(BF16), sbd=_block_diag([s64] * FNET_GROUPS).astype(BF16),
        m1=jnp.asarray(m1, F32).astype(BF16), m2=jnp.asarray(m2, F32).astype(BF16),
        tw_c=jnp.repeat(jnp.cos(th), FNET_W, axis=1), tw_s=jnp.repeat(jnp.sin(th), FNET_W, axis=1),
        hm=jnp.asarray(hm, BF16),
        ut=jnp.asarray(tt[:, None] < tt[None, :], BF16),
        lt=jnp.asarray(np.arange(N_EXPERTS)[None, :] < np.arange(N_EXPERTS)[:, None], F32))


def _layer_weights(l, p):
    w_in = p["w_in"][l]
    D = w_in.shape[0]
    o = np.cumsum([0, GLA_QK, GLA_QK, GLA_W, GLA_W, GLA_RANK, GLA_RANK, FNET_W, MLA_Q_RANK, MLA_KV_RANK, MLA_ROPE])
    col = lambda i: w_in[:, int(o[i]):int(o[i + 1])]
    zc = lambda n: jnp.zeros((D, n), F32)
    half = MLA_ROPE // 2
    kr = col(9)
    pad = MLA_QK_PAD - MLA_NOPE - MLA_ROPE
    kr1 = jnp.concatenate([zc(MLA_NOPE), kr[:, :half], kr[:, half:], zc(pad)], 1)
    kr2 = jnp.concatenate([zc(MLA_NOPE), kr[:, half:], kr[:, :half], zc(pad)], 1)
    w1 = jnp.concatenate([col(0), col(1), col(2), col(3), col(4), col(5), zc(128 - 2 * GLA_RANK), col(6), col(7),
                          col(8), kr1, kr2], 1).astype(BF16)
    wab = jnp.zeros((128, 2 * GLA_QK), F32)
    wab = wab.at[0:GLA_RANK, 0:GLA_QK].set(p["gla_wa_f"][l])
    wab = wab.at[GLA_RANK:2 * GLA_RANK, GLA_QK:].set(p["gla_wa_b"][l])
    bab = jnp.concatenate([p["gla_ba_f"][l], p["gla_ba_b"][l]])[None, :]
    wuq = p["mla_w_uq"][l].reshape(MLA_Q_RANK, MLA_HEADS, MLA_NOPE + MLA_ROPE)
    zq = lambda n: jnp.zeros((MLA_Q_RANK, MLA_HEADS, n), F32)
    nope, x1, x2 = wuq[..., :MLA_NOPE], wuq[..., MLA_NOPE:MLA_NOPE + half], wuq[..., MLA_NOPE + half:]
    wq1 = jnp.concatenate([nope, x1, x2, zq(pad)], -1).reshape(MLA_Q_RANK, -1)
    wq2 = jnp.concatenate([zq(MLA_NOPE), x2, x1, zq(pad)], -1).reshape(MLA_Q_RANK, -1)
    wukv = p["mla_w_ukv"][l].reshape(MLA_KV_RANK, MLA_HEADS, MLA_NOPE + MLA_V)
    wk = jnp.concatenate([wukv[..., :MLA_NOPE], jnp.zeros((MLA_KV_RANK, MLA_HEADS, MLA_QK_PAD - MLA_NOPE), F32)],
                         -1).reshape(MLA_KV_RANK, -1)
    wvt = wukv[..., MLA_NOPE:].reshape(MLA_KV_RANK, -1).T
    wr = jnp.zeros((256, D), F32)
    wr = wr.at[0:N_EXPERTS].set(p["moe_w_re"][l].T).at[128:128 + N_GROUPS].set(p["moe_w_rg"][l].T)
    br = jnp.zeros((256, 1), F32)
    br = br.at[0:N_EXPERTS, 0].set(p["moe_b_re"][l]).at[128:128 + N_GROUPS, 0].set(p["moe_b_rg"][l])
    wrh = wr.astype(BF16)
    return dict(
        g1=p["norm1_g"][l][None, :], w1=w1, wab=wab.astype(BF16), bab=bab,
        wmix=_block_diag([p["fnet_w"][l, g] for g in range(FNET_GROUPS)]).astype(BF16),
        qg=p["mla_q_norm_g"][l][None, :], wq1=wq1.T.astype(BF16), wq2=wq2.T.astype(BF16),
        kvg=p["mla_kv_norm_g"][l][None, :], wk=wk.astype(BF16), wvt=wvt.astype(BF16),
        gng=jnp.tile(p["gla_norm_g"][l], GLA_HEADS)[None, :], wo=p["w_out"][l].astype(BF16),
        n2g=p["norm2_g"][l][None, :], wrh=wrh, wrl=(wr - wrh.astype(F32)).astype(BF16), br=br,
        wg=p["moe_w_gate"][l].astype(BF16), wu=p["moe_w_up"][l].astype(BF16), wd=p["moe_w_down"][l].astype(BF16))


def _tiles(S):
    return dict(tm=min(512, S), tg=min(512, S), ts1=8, tk2=8, tq=min(512, S), tk=min(1024, S // 2), tb=min(1024, S))


def kernel(x, positions, norm1_g, w_in, gla_wa_f, gla_ba_f, gla_wa_b, gla_ba_b, gla_norm_g, fnet_w, mla_q_norm_g,
           mla_w_uq, mla_kv_norm_g, mla_w_ukv, w_out, norm2_g, moe_w_rg, moe_b_rg, moe_w_re, moe_b_re, moe_w_gate,
           moe_w_up, moe_w_down, final_norm_g):
    p = dict(norm1_g=norm1_g, w_in=w_in, gla_wa_f=gla_wa_f, gla_ba_f=gla_ba_f, gla_wa_b=gla_wa_b,
             gla_ba_b=gla_ba_b, gla_norm_g=gla_norm_g, fnet_w=fnet_w, mla_q_norm_g=mla_q_norm_g,
             mla_w_uq=mla_w_uq, mla_kv_norm_g=mla_kv_norm_g, mla_w_ukv=mla_w_ukv, w_out=w_out, norm2_g=norm2_g,
             moe_w_rg=moe_w_rg, moe_b_rg=moe_b_rg, moe_w_re=moe_w_re, moe_b_re=moe_b_re, moe_w_gate=moe_w_gate,
             moe_w_up=moe_w_up, moe_w_down=moe_w_down)
    B, S, D = x.shape
    t = _tiles(S)
    tabs = _tables(positions, S, t["tb"])
    depth = w_in.shape[0]
    for l in range(depth):
        lw = _layer_weights(l, p)
        gq, gk, gv, gg, la_f, la_b, zr, zi, qt, kk, vt = _proj(x, lw, tabs, t["tm"])
        o_f, o_b = _gla(gq, gk, gv, la_f, la_b, t["tg"])
        yf = _fnet(zr, zi, tabs, min(t["ts1"], tabs["n1"]), min(t["tk2"], tabs["n2"]))
        ot = _attn(qt, kk, vt, t["tq"], t["tk"])
        x1, h, info, off, cnt = _outproj(x, o_f, o_b, gg, yf, ot, lw, tabs, t["tb"])
        x = _moe(h.reshape(B * S, D), info, off[:, :, 0].reshape(-1), cnt[:, :, 0].reshape(-1), lw, t["tb"],
                 x1.reshape(B * S, D), final_norm_g[None, :] if l == depth - 1 else None).reshape(B, S, D)
    return x
```
